```python
import numpy as np
import jax
import jax.numpy as jnp
from jax import lax

D_MODEL = 2048
BATCH = 2
SEQ = 4096
DEPTH = 4

CHUNK = 64
RET_HEADS = 8
RET_DK = D_MODEL // 16
RET_DV = D_MODEL // RET_HEADS
GDN_HEADS = 16
GDN_DK = D_MODEL // GDN_HEADS
GDN_DV = D_MODEL // GDN_HEADS
CONV_K = 4
XATTN_HEADS = 4
XATTN_DH = D_MODEL // XATTN_HEADS
MEM_TOKENS = 256
D_FF = 4 * D_MODEL
ROPE_THETA = 10000.0
NORM_EPS = 1e-6
GDN_QKV = 2 * GDN_HEADS * GDN_DK + GDN_HEADS * GDN_DV
IN_SPLITS = (RET_HEADS * RET_DK, RET_HEADS * RET_DK, RET_HEADS * RET_DV, RET_HEADS * RET_DV,
             GDN_QKV, GDN_HEADS, GDN_HEADS, GDN_HEADS * GDN_DV, D_MODEL, D_MODEL)
IN_WIDTH = sum(IN_SPLITS)

kernel_name = "hybrid_retention_gdn_xattn_encoder"


def rms_norm(x, g):
    xf = x.astype(jnp.float32)
    y = xf * lax.rsqrt(jnp.mean(xf * xf, axis=-1, keepdims=True) + NORM_EPS)
    return (y * g.astype(jnp.float32)).astype(x.dtype)


def l2_normalize(t):
    return t * lax.rsqrt(jnp.sum(t * t, axis=-1, keepdims=True) + 1e-6)


def rope(t, positions):
    d = t.shape[-1]
    inv_freq = 1.0 / (ROPE_THETA ** (jnp.arange(0, d, 2, dtype=jnp.float32) / d))
    ang = positions.astype(jnp.float32)[..., None] * inv_freq
    cos = jnp.cos(ang)[:, :, None, :]
    sin = jnp.sin(ang)[:, :, None, :]
    t1, t2 = t[..., : d // 2], t[..., d // 2:]
    return jnp.concatenate([t1 * cos - t2 * sin, t1 * sin + t2 * cos], axis=-1)


def to_chunks(t):
    b, s, h, d = t.shape
    return t.reshape(b, s // CHUNK, CHUNK, h, d).transpose(1, 0, 3, 2, 4)


def from_chunks(t):
    n, b, h, c, d = t.shape
    return t.transpose(1, 0, 3, 2, 4).reshape(b, n * c, h, d)


def causal_conv(t, w):
    c = t.shape[-1]
    return lax.conv_general_dilated(
        t, w.astype(t.dtype)[:, None, :], window_strides=(1,),
        padding=[(CONV_K - 1, 0)], dimension_numbers=("NWC", "WIO", "NWC"),
        feature_group_count=c)


def retention_chunked(q, k, v):
    b, _, h, dk = q.shape
    dv = v.shape[-1]
    idx = jnp.arange(CHUNK, dtype=jnp.float32)
    log_gamma = jnp.log1p(-jnp.exp2(-5.0 - jnp.arange(RET_HEADS, dtype=jnp.float32)))
    intra_decay = jnp.exp(log_gamma[:, None, None] * jnp.abs(idx[:, None] - idx[None, :]))
    q_decay = jnp.exp(log_gamma[:, None] * (idx + 1.0))[:, :, None]
    k_decay = jnp.exp(log_gamma[:, None] * (CHUNK - 1.0 - idx))[:, :, None]
    chunk_decay = jnp.exp(log_gamma * CHUNK)[:, None, None]

    def step(state, xs):
        q_i, k_i, v_i = xs
        scores = jnp.einsum("bhid,bhjd->bhij", q_i, k_i) * intra_decay
        o = (jnp.einsum("bhij,bhje->bhie", scores, v_i)
             + jnp.einsum("bhid,bhde->bhie", q_i * q_decay, state))
        state = state * chunk_decay + jnp.einsum("bhjd,bhje->bhde", k_i * k_decay, v_i)
        return state, o

    s0 = jnp.zeros((b, h, dk, dv), jnp.float32)
    _, o = lax.scan(step, s0, (to_chunks(q), to_chunks(k), to_chunks(v)))
    return from_chunks(o)


def gated_delta_chunked(q, k, v, g, beta):
    b, _, h, dk = q.shape
    dv = v.shape[-1]
    qc, kc, vc = to_chunks(q), to_chunks(k), to_chunks(v)
    gc = to_chunks(g[..., None])[..., 0]
    bc = to_chunks(beta[..., None])
    cum_g = jnp.cumsum(gc, axis=-1)
    causal = jnp.tril(jnp.ones((CHUNK, CHUNK), dtype=bool))
    strict = jnp.tril(jnp.ones((CHUNK, CHUNK), dtype=bool), -1)
    decay = jnp.exp(jnp.where(causal, cum_g[..., :, None] - cum_g[..., None, :], -jnp.inf))
    kb = kc * bc
    eye = jnp.eye(CHUNK, dtype=jnp.float32)
    m = jnp.where(strict, jnp.einsum("nbhid,nbhjd->nbhij", kb, kc) * decay, 0.0)
    t = lax.linalg.triangular_solve(eye + m, jnp.broadcast_to(eye, m.shape),
                                    left_side=True, lower=True, unit_diagonal=True)
    u = t @ (vc * bc)
    w = t @ (kb * jnp.exp(cum_g)[..., None])
    qk = jnp.einsum("nbhid,nbhjd->nbhij", qc, kc) * decay

    def step(state, xs):
        q_i, k_i, u_i, w_i, g_i, qk_i = xs
        v_new = u_i - w_i @ state
        o = (q_i * jnp.exp(g_i)[..., None]) @ state + qk_i @ v_new
        g_last = g_i[..., -1:]
        state = (state * jnp.exp(g_last)[..., None]
                 + jnp.einsum("bhcd,bhce->bhde", k_i * jnp.exp(g_last - g_i)[..., None], v_new))
        return state, o

    s0 = jnp.zeros((b, h, dk, dv), jnp.float32)
    _, o = lax.scan(step, s0, (qc, kc, u, w, cum_g, qk))
    return from_chunks(o)


def hybrid_mixer(h, positions, w_in, conv_w, a_log, dt_bias, ret_gn_g, gdn_norm_g, w_out):
    f32 = jnp.float32
    b, s, _ = h.shape
    proj = h @ w_in
    (rq, rk, rv, rg, gqkv, ga, gb, gz, gate_a, gate_b) = jnp.split(
        proj, np.cumsum(IN_SPLITS)[:-1].tolist(), axis=-1)

    rq = rope(rq.astype(f32).reshape(b, s, RET_HEADS, RET_DK), positions)
    rk = rope(rk.astype(f32).reshape(b, s, RET_HEADS, RET_DK), positions) * (RET_DK ** -0.5)
    rv = rv.astype(f32).reshape(b, s, RET_HEADS, RET_DV)
    yr = retention_chunked(rq, rk, rv)
    mu = jnp.mean(yr, axis=-1, keepdims=True)
    var = jnp.mean(jnp.square(yr - mu), axis=-1, keepdims=True)
    yr = (yr - mu) * lax.rsqrt(var + 1e-5) * ret_gn_g.astype(f32).reshape(RET_HEADS, RET_DV)
    yr = jax.nn.silu(rg.astype(f32)) * yr.reshape(b, s, RET_HEADS * RET_DV)

    gqkv = jax.nn.silu(causal_conv(gqkv, conv_w)).astype(f32)
    gq, gk, gv = jnp.split(gqkv, [GDN_HEADS * GDN_DK, 2 * GDN_HEADS * GDN_DK], axis=-1)
    gq = l2_normalize(gq.reshape(b, s, GDN_HEADS, GDN_DK)) * (GDN_DK ** -0.5)
    gk = l2_normalize(gk.reshape(b, s, GDN_HEADS, GDN_DK))
    gv = gv.reshape(b, s, GDN_HEADS, GDN_DV)
    beta = jax.nn.sigmoid(gb.astype(f32))
    log_a = -jnp.exp(a_log.astype(f32)) * jax.nn.softplus(ga.astype(f32) + dt_bias.astype(f32))
    yg = gated_delta_chunked(gq, gk, gv, log_a, beta)
    yg = yg * lax.rsqrt(jnp.mean(yg * yg, axis=-1, keepdims=True) + NORM_EPS) * gdn_norm_g.astype(f32)
    yg = jax.nn.silu(gz.astype(f32)) * yg.reshape(b, s, GDN_HEADS * GDN_DV)

    merged = jax.nn.sigmoid(gate_a.astype(f32)) * yr + jax.nn.sigmoid(gate_b.astype(f32)) * yg
    return merged.astype(h.dtype) @ w_out


def cross_attention(h, mem_n, w_q, w_kv, w_o):
    f32 = jnp.float32
    b, s, _ = h.shape
    q = (h @ w_q).reshape(b, s, XATTN_HEADS, XATTN_DH).astype(f32)
    k, v = jnp.split(mem_n @ w_kv, 2, axis=-1)
    k = k.reshape(b, -1, XATTN_HEADS, XATTN_DH).astype(f32)
    v = v.reshape(b, -1, XATTN_HEADS, XATTN_DH).astype(f32)
    scores = jnp.einsum("bshd,bmhd->bhsm", q, k) * (XATTN_DH ** -0.5)
    p = jax.nn.softmax(scores, axis=-1)
    o = jnp.einsum("bhsm,bmhd->bshd", p, v).reshape(b, s, D_MODEL).astype(h.dtype)
    return o @ w_o


def setup_inputs(seed: int = 0) -> dict:
    key = jax.random.key(seed)
    ks = jax.random.split(key, 24)
    f32 = jnp.float32

    def dense(k, shape, fan_in):
        return jax.random.normal(k, shape, f32) * (fan_in ** -0.5)

    def gain(k, shape):
        return 1.0 + 0.02 * jax.random.normal(k, shape, f32)

    dt = jnp.exp(jax.random.uniform(ks[5], (DEPTH, GDN_HEADS), f32)
                 * (jnp.log(0.1) - jnp.log(0.001)) + jnp.log(0.001))
    return {
        "x": jax.random.normal(ks[0], (BATCH, SEQ, D_MODEL), f32),
        "mem": jax.random.normal(ks[1], (BATCH, MEM_TOKENS, D_MODEL), f32),
        "positions": jnp.broadcast_to(jnp.arange(SEQ, dtype=jnp.int32), (BATCH, SEQ)),
        "norm_mix_g": gain(ks[2], (DEPTH, D_MODEL)),
        "w_in": dense(ks[3], (DEPTH, D_MODEL, IN_WIDTH), D_MODEL),
        "conv_w": dense(ks[4], (DEPTH, CONV_K, GDN_QKV), CONV_K),
        "gdn_a_log": jnp.log(jax.random.uniform(ks[6], (DEPTH, GDN_HEADS), f32, 1.0, 16.0)),
        "gdn_dt_bias": dt + jnp.log(-jnp.expm1(-dt)),
        "ret_gn_g": gain(ks[7], (DEPTH, RET_HEADS * RET_DV)),
        "gdn_norm_g": gain(ks[8], (DEPTH, GDN_DV)),
        "w_out": dense(ks[9], (DEPTH, D_MODEL, D_MODEL), D_MODEL),
        "norm_x_g": gain(ks[10], (DEPTH, D_MODEL)),
        "norm_mem_g": gain(ks[11], (DEPTH, D_MODEL)),
        "w_xq": dense(ks[12], (DEPTH, D_MODEL, D_MODEL), D_MODEL),
        "w_xkv": dense(ks[13], (DEPTH, D_MODEL, 2 * D_MODEL), D_MODEL),
        "w_xo": dense(ks[14], (DEPTH, D_MODEL, D_MODEL), D_MODEL),
        "norm_ffn_g": gain(ks[15], (DEPTH, D_MODEL)),
        "w_ff1": dense(ks[16], (DEPTH, D_MODEL, D_FF), D_MODEL),
        "w_ff2": dense(ks[17], (DEPTH, D_FF, D_MODEL), D_FF),
        "norm_final_g": gain(ks[18], (D_MODEL,)),
    }


def reference(x, mem, positions, norm_mix_g, w_in, conv_w, gdn_a_log, gdn_dt_bias,
              ret_gn_g, gdn_norm_g, w_out, norm_x_g, norm_mem_g, w_xq, w_xkv, w_xo,
              norm_ffn_g, w_ff1, w_ff2, norm_final_g):
    for l in range(DEPTH):
        h = rms_norm(x, norm_mix_g[l])
        x = x + hybrid_mixer(h, positions, w_in[l], conv_w[l], gdn_a_log[l], gdn_dt_bias[l],
                             ret_gn_g[l], gdn_norm_g[l], w_out[l])
        h = rms_norm(x, norm_x_g[l])
        mem_n = rms_norm(mem, norm_mem_g[l])
        x = x + cross_attention(h, mem_n, w_xq[l], w_xkv[l], w_xo[l])
        h = rms_norm(x, norm_ffn_g[l])
        x = x + jnp.square(jax.nn.relu(h @ w_ff1[l])) @ w_ff2[l]
    return rms_norm(x, norm_final_g)
```

```python
import functools

import numpy as np
import jax
import jax.numpy as jnp
from jax import lax
from jax.experimental import pallas as pl
from jax.experimental.pallas import tpu as pltpu

F32 = jnp.float32
BF16 = jnp.bfloat16

D_MODEL = 2048
CHUNK = 64
RET_HEADS = 8
RET_DK = D_MODEL // 16
RET_DV = D_MODEL // RET_HEADS
GDN_HEADS = 16
GDN_DK = D_MODEL // GDN_HEADS
GDN_DV = D_MODEL // GDN_HEADS
CONV_K = 4
XATTN_HEADS = 4
XATTN_DH = D_MODEL // XATTN_HEADS
D_FF = 4 * D_MODEL
ROPE_THETA = 10000.0
NORM_EPS = 1e-6
GDN_QKV = 3 * GDN_HEADS * GDN_DK
OFF_RQ = 0
OFF_RK = OFF_RQ + RET_HEADS * RET_DK
OFF_RV = OFF_RK + RET_HEADS * RET_DK
OFF_RG = OFF_RV + RET_HEADS * RET_DV
OFF_GQKV = OFF_RG + RET_HEADS * RET_DV
OFF_GA = OFF_GQKV + GDN_QKV
OFF_GB = OFF_GA + GDN_HEADS
OFF_GZ = OFF_GB + GDN_HEADS
OFF_GATE_A = OFF_GZ + GDN_HEADS * GDN_DV
OFF_GATE_B = OFF_GATE_A + D_MODEL
IN_WIDTH = OFF_GATE_B + D_MODEL
MAIN_WIDTH = OFF_GA

LANE = 128
RET_BLOCK = 256
VMEM_LIMIT = 56 * 1024 * 1024


def _cparams(sem):
    return pltpu.CompilerParams(dimension_semantics=sem, vmem_limit_bytes=VMEM_LIMIT)


def _dot(a, b):
    return jnp.dot(a, b, preferred_element_type=F32)


def _dot_nt(a, b):
    return lax.dot_general(a, b, (((1,), (1,)), ((), ())), preferred_element_type=F32)


def _dot_tn(a, b):
    return lax.dot_general(a, b, (((0,), (0,)), ((), ())), preferred_element_type=F32)


def _sigmoid(x):
    return 1.0 / (1.0 + jnp.exp(-x))


def _silu(x):
    return x * _sigmoid(x)


def _rmsnorm_kernel(x_ref, g_ref, o_ref):
    x = x_ref[...]
    ms = jnp.mean(x * x, axis=-1, keepdims=True)
    o_ref[...] = (x * lax.rsqrt(ms + NORM_EPS) * g_ref[...]).astype(o_ref.dtype)


def _rmsnorm(x, g, out_dtype, tm=512):
    m, d = x.shape
    tm = min(tm, m)
    return pl.pallas_call(
        _rmsnorm_kernel,
        grid=(m // tm,),
        in_specs=[pl.BlockSpec((tm, d), lambda i: (i, 0)),
                  pl.BlockSpec((1, d), lambda i: (0, 0))],
        out_specs=pl.BlockSpec((tm, d), lambda i: (i, 0)),
        out_shape=jax.ShapeDtypeStruct((m, d), out_dtype),
        compiler_params=_cparams(("parallel",)),
        name="rmsnorm",
    )(x, g.reshape(1, d))


def _mm_kernel(*refs, relu2, has_res):
    if has_res:
        a_ref, w_ref, r_ref, o_ref, wbf_ref = refs
    else:
        a_ref, w_ref, o_ref, wbf_ref = refs

    @pl.when(pl.program_id(1) == 0)
    def _():
        wbf_ref[...] = w_ref[...].astype(BF16)

    acc = _dot(a_ref[...], wbf_ref[...])
    if relu2:
        acc = jnp.square(jnp.maximum(acc, 0.0))
    if has_res:
        acc = acc + r_ref[...]
    o_ref[...] = acc.astype(o_ref.dtype)


def _matmul(a, w, layer, n_out, out_dtype, tm, tn, relu2=False, residual=None):
    m, k = a.shape
    assert w.shape[1] == k and n_out % tn == 0 and m % tm == 0
    in_specs = [pl.BlockSpec((tm, k), lambda j, i: (i, 0)),
                pl.BlockSpec((None, k, tn), lambda j, i: (layer, 0, j))]
    args = [a, w]
    if residual is not None:
        in_specs.append(pl.BlockSpec((tm, tn), lambda j, i: (i, j)))
        args.append(residual)
    return pl.pallas_call(
        functools.partial(_mm_kernel, relu2=relu2, has_res=residual is not None),
        grid=(n_out // tn, m // tm),
        in_specs=in_specs,
        out_specs=pl.BlockSpec((tm, tn), lambda j, i: (i, j)),
        out_shape=jax.ShapeDtypeStruct((m, n_out), out_dtype),
        scratch_shapes=[pltpu.VMEM((k, tn), BF16)],
        compiler_params=_cparams(("arbitrary", "arbitrary")),
        name="matmul",
    )(*args)


def _rope_kernel(pos_ref, invf_ref, sgn_ref, cos_ref, sin_ref):
    ang = pos_ref[...] * invf_ref[...]
    cos_ref[...] = jnp.cos(ang)
    sin_ref[...] = jnp.sin(ang) * sgn_ref[...]


def _rope_tables(positions):
    t = positions.size
    half = RET_DK // 2
    inv_freq = 1.0 / (ROPE_THETA ** (jnp.arange(0, RET_DK, 2, dtype=F32) / RET_DK))
    invf2 = jnp.concatenate([inv_freq, inv_freq]).reshape(1, RET_DK)
    sgn = jnp.concatenate([-jnp.ones((half,), F32), jnp.ones((half,), F32)]).reshape(1, RET_DK)
    pos = positions.astype(F32).reshape(t, 1)
    tm = 1024
    return pl.pallas_call(
        _rope_kernel,
        grid=(t // tm,),
        in_specs=[pl.BlockSpec((tm, 1), lambda i: (i, 0)),
                  pl.BlockSpec((1, RET_DK), lambda i: (0, 0)),
                  pl.BlockSpec((1, RET_DK), lambda i: (0, 0))],
        out_specs=[pl.BlockSpec((tm, RET_DK), lambda i: (i, 0))] * 2,
        out_shape=[jax.ShapeDtypeStruct((t, RET_DK), F32)] * 2,
        compiler_params=_cparams(("parallel",)),
        name="rope_tables",
    )(pos, invf2, sgn)


def _retention_tables():
    idx = np.arange(RET_BLOCK, dtype=np.float64)
    log_gamma = jnp.log1p(-jnp.exp2(-5.0 - jnp.arange(RET_HEADS, dtype=F32)))
    lg = log_gamma[:, None, None]
    dist = jnp.asarray(np.abs(idx[:, None] - idx[None, :]), F32)
    visible = jnp.asarray((idx[None, :] // CHUNK) <= (idx[:, None] // CHUNK))
    wmat = jnp.where(visible[None], jnp.exp(lg * dist[None]), 0.0)
    qd = jnp.exp(log_gamma[:, None] * jnp.asarray(idx + 1.0, F32))
    kd = jnp.exp(log_gamma[:, None] * jnp.asarray(RET_BLOCK - 1.0 - idx, F32))
    cd = jnp.exp(log_gamma * RET_BLOCK)
    qd = jnp.broadcast_to(qd[:, :, None], (RET_HEADS, RET_BLOCK, RET_DK))
    kd = jnp.broadcast_to(kd[:, :, None], (RET_HEADS, RET_BLOCK, RET_DK))
    cd = jnp.broadcast_to(cd[:, None, None], (RET_HEADS, 8, RET_DV))
    return wmat, qd, kd, cd


def _retention_kernel(q_ref, k_ref, v_ref, rg_ref, cos_ref, sin_ref, w_ref, qd_ref, kd_ref,
                      cd_ref, gn_ref, o_ref, state_ref):
    @pl.when(pl.program_id(2) == 0)
    def _():
        state_ref[...] = jnp.zeros_like(state_ref)

    cos = cos_ref[...]
    sin = sin_ref[...]
    q = q_ref[...]
    k = k_ref[...]
    half = RET_DK // 2
    qr = q * cos + pltpu.roll(q, half, axis=1) * sin
    kr = (k * cos + pltpu.roll(k, half, axis=1) * sin) * (RET_DK ** -0.5)
    vb = v_ref[...].astype(BF16)
    scores = _dot_nt(qr.astype(BF16), kr.astype(BF16)) * w_ref[...]
    s_prev = state_ref[...]
    o = _dot(scores.astype(BF16), vb) + _dot((qr * qd_ref[...]).astype(BF16), s_prev.astype(BF16))
    state_ref[...] = s_prev * cd_ref[0:1, :] + _dot_tn((kr * kd_ref[...]).astype(BF16), vb)
    mu = jnp.mean(o, axis=-1, keepdims=True)
    oc = o - mu
    var = jnp.mean(oc * oc, axis=-1, keepdims=True)
    y = oc * lax.rsqrt(var + 1e-5) * gn_ref[...]
    o_ref[...] = _silu(rg_ref[...]) * y


def _retention(proj, cos2, sin2, tables, gn_g, batch, seq):
    wmat, qd, kd, cd = tables
    t = batch * seq
    nb = seq // RET_BLOCK
    L = RET_BLOCK

    def row(b, h, s):
        return b * nb + s

    return pl.pallas_call(
        _retention_kernel,
        grid=(batch, RET_HEADS, nb),
        in_specs=[
            pl.BlockSpec((L, RET_DK), lambda b, h, s: (row(b, h, s), OFF_RQ // RET_DK + h)),
            pl.BlockSpec((L, RET_DK), lambda b, h, s: (row(b, h, s), OFF_RK // RET_DK + h)),
            pl.BlockSpec((L, RET_DV), lambda b, h, s: (row(b, h, s), OFF_RV // RET_DV + h)),
            pl.BlockSpec((L, RET_DV), lambda b, h, s: (row(b, h, s), OFF_RG // RET_DV + h)),
            pl.BlockSpec((L, RET_DK), lambda b, h, s: (row(b, h, s), 0)),
            pl.BlockSpec((L, RET_DK), lambda b, h, s: (row(b, h, s), 0)),
            pl.BlockSpec((None, L, L), lambda b, h, s: (h, 0, 0)),
            pl.BlockSpec((None, L, RET_DK), lambda b, h, s: (h, 0, 0)),
            pl.BlockSpec((None, L, RET_DK), lambda b, h, s: (h, 0, 0)),
            pl.BlockSpec((None, 8, RET_DV), lambda b, h, s: (h, 0, 0)),
            pl.BlockSpec((1, RET_DV), lambda b, h, s: (0, h)),
        ],
        out_specs=pl.BlockSpec((L, RET_DV), lambda b, h, s: (row(b, h, s), h)),
        out_shape=jax.ShapeDtypeStruct((t, RET_HEADS * RET_DV), F32),
        scratch_shapes=[pltpu.VMEM((RET_DK, RET_DV), F32)],
        compiler_params=_cparams(("arbitrary", "arbitrary", "arbitrary")),
        name="retention",
    )(proj, proj, proj, proj, cos2, sin2, wmat, qd, kd, cd, gn_g.reshape(1, -1))


def _cumsum_rows(x):
    row = lax.broadcasted_iota(jnp.int32, x.shape, 0)
    s = 1
    while s < x.shape[0]:
        x = x + jnp.where(row >= s, pltpu.roll(x, s, axis=0), 0.0)
        s *= 2
    return x


def _softplus(x):
    return jnp.maximum(x, 0.0) + jnp.log1p(jnp.exp(-jnp.abs(x)))


def _gdn_kernel(qkv_ref, small_ref, convw_ref, alog_ref, dtb_ref, gz_ref, ga_ref, gb_ref,
                yr_ref, gng_ref, o_ref, state_ref, xext_ref):
    C = CHUNK
    HALO = 8

    @pl.when(pl.program_id(1) == 0)
    def _():
        state_ref[...] = jnp.zeros_like(state_ref)
        xext_ref[0:HALO, :] = jnp.zeros((HALO, GDN_QKV), F32)

    xext_ref[HALO:HALO + C, :] = qkv_ref[...]

    sm = small_ref[...]
    gl = -jnp.exp(alog_ref[...]) * _softplus(sm + dtb_ref[...])
    cg = _cumsum_rows(gl)
    cg_t = cg.T
    bt = _sigmoid(sm)

    ii = lax.broadcasted_iota(jnp.int32, (C, C), 0)
    jj = lax.broadcasted_iota(jnp.int32, (C, C), 1)
    causal = ii >= jj
    strict = ii > jj
    gng = gng_ref[...]
    lvl_masks = []
    for b in range(C.bit_length() - 1):
        ib = lax.shift_right_logical(ii, b)
        jb = lax.shift_right_logical(jj, b)
        lvl_masks.append(((ib & 1) == 1) & (jb == ib - 1))

    def conv_act(col0):
        sl = slice(col0, col0 + LANE)
        y = convw_ref[CONV_K - 1:CONV_K, sl] * xext_ref[HALO:HALO + C, sl]
        for d in range(1, CONV_K):
            y = y + convw_ref[CONV_K - 1 - d:CONV_K - d, sl] * xext_ref[HALO - d:HALO - d + C, sl]
        return _silu(y)

    for h in range(GDN_HEADS):
        hs = slice(h * GDN_DV, (h + 1) * GDN_DV)
        q = conv_act(h * GDN_DK)
        k = conv_act(GDN_HEADS * GDN_DK + h * GDN_DK)
        v = conv_act(2 * GDN_HEADS * GDN_DK + h * GDN_DV)
        qn = q * lax.rsqrt(jnp.sum(q * q, axis=-1, keepdims=True) + 1e-6) * (GDN_DK ** -0.5)
        kn = k * lax.rsqrt(jnp.sum(k * k, axis=-1, keepdims=True) + 1e-6)

        cg_c = jnp.broadcast_to(cg[:, h:h + 1], (C, LANE))
        bt_c = jnp.broadcast_to(bt[:, GDN_HEADS + h:GDN_HEADS + h + 1], (C, LANE))
        cg_r = cg_t[h:h + 1, :]
        ecg = jnp.exp(cg_c)
        decay = jnp.where(causal, jnp.exp(jnp.where(causal, cg_c[:, :C] - cg_r, 0.0)), 0.0)

        kb = kn * bt_c
        knb = kn.astype(BF16)
        kq = _dot_nt(jnp.concatenate([kb, qn], axis=0).astype(BF16), knb)
        m = jnp.where(strict, kq[:C] * decay, 0.0)
        qk = kq[C:] * decay

        r = jnp.where(lvl_masks[0], -m, 0.0)
        for lm in lvl_masks[1:]:
            cm = jnp.where(lm, m, 0.0)
            y = cm + _dot(r.astype(BF16), cm.astype(BF16))
            r = r - (y + _dot(y.astype(BF16), r.astype(BF16)))

        xcat = jnp.concatenate([v * bt_c, kb * ecg], axis=1)
        uw = xcat + _dot(r.astype(BF16), xcat.astype(BF16))
        u = uw[:, :GDN_DV]
        w = uw[:, GDN_DV:]

        s_prev = state_ref[h]
        ws_qs = _dot(jnp.concatenate([w, qn * ecg], axis=0).astype(BF16), s_prev.astype(BF16))
        v_new = u - ws_qs[:C]
        vnb = v_new.astype(BF16)
        o = ws_qs[C:] + _dot(qk.astype(BF16), vnb)
        g_last = cg_c[C - 1:C, :]
        kdec = kn * jnp.exp(g_last - cg_c)
        state_ref[h] = s_prev * jnp.exp(g_last) + _dot_tn(kdec.astype(BF16), vnb)

        on = o * lax.rsqrt(jnp.mean(o * o, axis=-1, keepdims=True) + NORM_EPS) * gng
        yg = _silu(gz_ref[:, hs]) * on
        merged = _sigmoid(ga_ref[:, hs]) * yr_ref[:, hs] + _sigmoid(gb_ref[:, hs]) * yg
        o_ref[:, hs] = merged.astype(o_ref.dtype)

    xext_ref[0:HALO, :] = xext_ref[C:C + HALO, :]


def _gdn_merge(proj, small, tail, yr, conv_w, a_log, dt_bias, gdn_norm_g, batch, seq):
    t = batch * seq
    nc = seq // CHUNK
    C = CHUNK
    pad = LANE - GDN_HEADS
    alog_p = jnp.pad(a_log.astype(F32), (0, pad)).reshape(1, LANE)
    dtb_p = jnp.pad(dt_bias.astype(F32), (0, pad)).reshape(1, LANE)

    def row(b, c):
        return b * nc + c

    return pl.pallas_call(
        _gdn_kernel,
        grid=(batch, nc),
        in_specs=[
            pl.BlockSpec((C, GDN_QKV), lambda b, c: (row(b, c), OFF_GQKV // GDN_QKV)),
            pl.BlockSpec((C, LANE), lambda b, c: (row(b, c), 0)),
            pl.BlockSpec((CONV_K, GDN_QKV), lambda b, c: (0, 0)),
            pl.BlockSpec((1, LANE), lambda b, c: (0, 0)),
            pl.BlockSpec((1, LANE), lambda b, c: (0, 0)),
            pl.BlockSpec((C, D_MODEL), lambda b, c: (row(b, c), 0)),
            pl.BlockSpec((C, D_MODEL), lambda b, c: (row(b, c), 1)),
            pl.BlockSpec((C, D_MODEL), lambda b, c: (row(b, c), 2)),
            pl.BlockSpec((C, D_MODEL), lambda b, c: (row(b, c), 0)),
            pl.BlockSpec((1, GDN_DV), lambda b, c: (0, 0)),
        ],
        out_specs=pl.BlockSpec((C, D_MODEL), lambda b, c: (row(b, c), 0)),
        out_shape=jax.ShapeDtypeStruct((t, D_MODEL), BF16),
        scratch_shapes=[pltpu.VMEM((GDN_HEADS, GDN_DK, GDN_DV), F32),
                        pltpu.VMEM((C + 8, GDN_QKV), F32)],
        compiler_params=_cparams(("arbitrary", "arbitrary")),
        name="gdn_merge",
    )(proj, small, conv_w, alog_p, dtb_p, tail, tail, tail, yr, gdn_norm_g.reshape(1, GDN_DV))


def _xattn_kernel(q_ref, k_ref, v_ref, o_ref):
    s = _dot_nt(q_ref[...], k_ref[...]) * (XATTN_DH ** -0.5)
    p = jnp.exp(s - jnp.max(s, axis=-1, keepdims=True))
    l = jnp.sum(p, axis=-1, keepdims=True)
    o = _dot(p.astype(BF16), v_ref[...]) / l
    o_ref[...] = o.astype(o_ref.dtype)


def _xattn(q, kv, batch, seq, mem_tokens, tq=512):
    t = batch * seq
    nq = seq // tq
    return pl.pallas_call(
        _xattn_kernel,
        grid=(batch, nq, XATTN_HEADS),
        in_specs=[
            pl.BlockSpec((tq, XATTN_DH), lambda b, i, h: (b * nq + i, h)),
            pl.BlockSpec((mem_tokens, XATTN_DH), lambda b, i, h: (b, h)),
            pl.BlockSpec((mem_tokens, XATTN_DH), lambda b, i, h: (b, XATTN_HEADS + h)),
        ],
        out_specs=pl.BlockSpec((tq, XATTN_DH), lambda b, i, h: (b * nq + i, h)),
        out_shape=jax.ShapeDtypeStruct((t, D_MODEL), BF16),
        compiler_params=_cparams(("parallel", "parallel", "parallel")),
        name="xattn",
    )(q, kv, kv)


def kernel(x, mem, positions, norm_mix_g, w_in, conv_w, gdn_a_log, gdn_dt_bias, ret_gn_g, gdn_norm_g, w_out, norm_x_g, norm_mem_g, w_xq, w_xkv, w_xo, norm_ffn_g, w_ff1, w_ff2, norm_final_g):
    batch, seq, d = x.shape
    mem_tokens = mem.shape[1]
    depth = w_in.shape[0]
    t = batch * seq
    x2 = x.reshape(t, d)
    mem2 = mem.reshape(batch * mem_tokens, d)

    cos2, sin2 = _rope_tables(positions)
    ret_tables = _retention_tables()

    w_tail = w_in[:, :, OFF_GZ:]
    w_small = jnp.pad(w_in[:, :, OFF_GA:OFF_GZ], ((0, 0), (0, 0), (0, LANE - 2 * GDN_HEADS)))

    for l in range(depth):
        h = _rmsnorm(x2, norm_mix_g[l], BF16)
        proj = _matmul(h, w_in, l, MAIN_WIDTH, F32, tm=1024, tn=1024)
        tail = _matmul(h, w_tail, l, 3 * D_MODEL, F32, tm=1024, tn=1024)
        small = _matmul(h, w_small, l, LANE, F32, tm=1024, tn=LANE)
        yr = _retention(proj, cos2, sin2, ret_tables, ret_gn_g[l], batch, seq)
        merged = _gdn_merge(proj, small, tail, yr, conv_w[l], gdn_a_log[l], gdn_dt_bias[l],
                            gdn_norm_g[l], batch, seq)
        x2 = _matmul(merged, w_out, l, d, F32, tm=1024, tn=1024, residual=x2)

        h = _rmsnorm(x2, norm_x_g[l], BF16)
        mem_n = _rmsnorm(mem2, norm_mem_g[l], BF16)
        q = _matmul(h, w_xq, l, d, BF16, tm=1024, tn=1024)
        kv = _matmul(mem_n, w_xkv, l, 2 * d, BF16, tm=batch * mem_tokens, tn=1024)
        o = _xattn(q, kv, batch, seq, mem_tokens)
        x2 = _matmul(o, w_xo, l, d, F32, tm=1024, tn=1024, residual=x2)

        h = _rmsnorm(x2, norm_ffn_g[l], BF16)
        hid = _matmul(h, w_ff1, l, D_FF, BF16, tm=1024, tn=1024, relu2=True)
        x2 = _matmul(hid, w_ff2, l, d, F32, tm=512, tn=256, residual=x2)

    out = _rmsnorm(x2, norm_final_g, x.dtype)
    return out.reshape(batch, seq, d)
```

```python
import functools

import numpy as np
import jax
import jax.numpy as jnp
from jax import lax
from jax.experimental import pallas as pl
from jax.experimental.pallas import tpu as pltpu

F32 = jnp.float32
BF16 = jnp.bfloat16

D_MODEL = 2048
CHUNK = 64
RET_HEADS = 8
RET_DK = D_MODEL // 16
RET_DV = D_MODEL // RET_HEADS
GDN_HEADS = 16
GDN_DK = D_MODEL // GDN_HEADS
GDN_DV = D_MODEL // GDN_HEADS
CONV_K = 4
XATTN_HEADS = 4
XATTN_DH = D_MODEL // XATTN_HEADS
D_FF = 4 * D_MODEL
ROPE_THETA = 10000.0
NORM_EPS = 1e-6
GDN_QKV = 3 * GDN_HEADS * GDN_DK
OFF_RQ = 0
OFF_RK = OFF_RQ + RET_HEADS * RET_DK
OFF_RV = OFF_RK + RET_HEADS * RET_DK
OFF_RG = OFF_RV + RET_HEADS * RET_DV
OFF_GQKV = OFF_RG + RET_HEADS * RET_DV
OFF_GA = OFF_GQKV + GDN_QKV
OFF_GB = OFF_GA + GDN_HEADS
OFF_GZ = OFF_GB + GDN_HEADS
OFF_GATE_A = OFF_GZ + GDN_HEADS * GDN_DV
OFF_GATE_B = OFF_GATE_A + D_MODEL
IN_WIDTH = OFF_GATE_B + D_MODEL
MAIN_WIDTH = (IN_WIDTH // 1024) * 1024

LANE = 128
RET_BLOCK = 256
VMEM_LIMIT = 56 * 1024 * 1024


def _cparams(sem):
    return pltpu.CompilerParams(dimension_semantics=sem, vmem_limit_bytes=VMEM_LIMIT)


def _dot(a, b):
    return jnp.dot(a, b, preferred_element_type=F32)


def _dot_nt(a, b):
    return lax.dot_general(a, b, (((1,), (1,)), ((), ())), preferred_element_type=F32)


def _dot_tn(a, b):
    return lax.dot_general(a, b, (((0,), (0,)), ((), ())), preferred_element_type=F32)


def _sigmoid(x):
    return 1.0 / (1.0 + jnp.exp(-x))


def _silu(x):
    return x * _sigmoid(x)


def _rmsnorm_kernel(x_ref, g_ref, o_ref):
    x = x_ref[...]
    ms = jnp.mean(x * x, axis=-1, keepdims=True)
    o_ref[...] = (x * lax.rsqrt(ms + NORM_EPS) * g_ref[...]).astype(o_ref.dtype)


def _rmsnorm(x, g, out_dtype, tm=512):
    m, d = x.shape
    tm = min(tm, m)
    return pl.pallas_call(
        _rmsnorm_kernel,
        grid=(m // tm,),
        in_specs=[pl.BlockSpec((tm, d), lambda i: (i, 0)),
                  pl.BlockSpec((1, d), lambda i: (0, 0))],
        out_specs=pl.BlockSpec((tm, d), lambda i: (i, 0)),
        out_shape=jax.ShapeDtypeStruct((m, d), out_dtype),
        compiler_params=_cparams(("parallel",)),
        name="rmsnorm",
    )(x, g.reshape(1, d))


def _mm_kernel(*refs, relu2, has_res, cast_w):
    refs = list(refs)
    a_ref, w_ref = refs[:2]
    r_ref = refs[2] if has_res else None
    o_ref = refs[3] if has_res else refs[2]
    if cast_w:
        wbf_ref = refs[-1]

        @pl.when(pl.program_id(1) == 0)
        def _():
            wbf_ref[...] = w_ref[...].astype(BF16)
    else:
        wbf_ref = w_ref

    acc = _dot(a_ref[...], wbf_ref[...])
    if relu2:
        acc = jnp.square(jnp.maximum(acc, 0.0))
    if has_res:
        acc = acc + r_ref[...]
    o_ref[...] = acc.astype(o_ref.dtype)


def _matmul(a, w, layer, n_out, out_dtype, tm, tn, relu2=False, residual=None):
    m, k = a.shape
    assert w.shape[1] == k and n_out % tn == 0 and m % tm == 0
    cast_w = w.dtype != BF16
    in_specs = [pl.BlockSpec((tm, k), lambda j, i: (i, 0)),
                pl.BlockSpec((None, k, tn), lambda j, i: (layer, 0, j))]
    args = [a, w]
    if residual is not None:
        in_specs.append(pl.BlockSpec((tm, tn), lambda j, i: (i, j)))
        args.append(residual)
    return pl.pallas_call(
        functools.partial(_mm_kernel, relu2=relu2, has_res=residual is not None, cast_w=cast_w),
        grid=(n_out // tn, m // tm),
        in_specs=in_specs,
        out_specs=pl.BlockSpec((tm, tn), lambda j, i: (i, j)),
        out_shape=jax.ShapeDtypeStruct((m, n_out), out_dtype),
        scratch_shapes=[pltpu.VMEM((k, tn), BF16)] if cast_w else [],
        compiler_params=_cparams(("arbitrary", "arbitrary")),
        name="matmul",
    )(*args)


def _rope_kernel(pos_ref, invf_ref, sgn_ref, cos_ref, sin_ref):
    ang = pos_ref[...] * invf_ref[...]
    cos_ref[...] = jnp.cos(ang)
    sin_ref[...] = jnp.sin(ang) * sgn_ref[...]


def _rope_tables(positions):
    t = positions.size
    half = RET_DK // 2
    inv_freq = 1.0 / (ROPE_THETA ** (jnp.arange(0, RET_DK, 2, dtype=F32) / RET_DK))
    invf2 = jnp.concatenate([inv_freq, inv_freq]).reshape(1, RET_DK)
    sgn = jnp.concatenate([-jnp.ones((half,), F32), jnp.ones((half,), F32)]).reshape(1, RET_DK)
    pos = positions.astype(F32).reshape(t, 1)
    tm = 1024
    return pl.pallas_call(
        _rope_kernel,
        grid=(t // tm,),
        in_specs=[pl.BlockSpec((tm, 1), lambda i: (i, 0)),
                  pl.BlockSpec((1, RET_DK), lambda i: (0, 0)),
                  pl.BlockSpec((1, RET_DK), lambda i: (0, 0))],
        out_specs=[pl.BlockSpec((tm, RET_DK), lambda i: (i, 0))] * 2,
        out_shape=[jax.ShapeDtypeStruct((t, RET_DK), F32)] * 2,
        compiler_params=_cparams(("parallel",)),
        name="rope_tables",
    )(pos, invf2, sgn)


def _retention_tables():
    idx = np.arange(RET_BLOCK, dtype=np.float64)
    log_gamma = jnp.log1p(-jnp.exp2(-5.0 - jnp.arange(RET_HEADS, dtype=F32)))
    lg = log_gamma[:, None, None]
    dist = jnp.asarray(np.abs(idx[:, None] - idx[None, :]), F32)
    visible = jnp.asarray((idx[None, :] // CHUNK) <= (idx[:, None] // CHUNK))
    wmat = jnp.where(visible[None], jnp.exp(lg * dist[None]), 0.0)
    qd = jnp.exp(log_gamma[:, None] * jnp.asarray(idx + 1.0, F32))
    kd = jnp.exp(log_gamma[:, None] * jnp.asarray(RET_BLOCK - 1.0 - idx, F32))
    cd = jnp.exp(log_gamma * RET_BLOCK)
    qd = jnp.broadcast_to(qd[:, :, None], (RET_HEADS, RET_BLOCK, RET_DK))
    kd = jnp.broadcast_to(kd[:, :, None], (RET_HEADS, RET_BLOCK, RET_DK))
    cd = jnp.broadcast_to(cd[:, None, None], (RET_HEADS, 8, RET_DV))
    return wmat, qd, kd, cd


def _retention_kernel(q_ref, k_ref, v_ref, rg_ref, cos_ref, sin_ref, w_ref, qd_ref, kd_ref,
                      cd_ref, gn_ref, o_ref, state_ref):
    @pl.when(pl.program_id(1) == 0)
    def _():
        state_ref[...] = jnp.zeros_like(state_ref)

    cos = cos_ref[...]
    sin = sin_ref[...]
    half = RET_DK // 2
    heads = range(RET_HEADS)
    qr, kr, vb = [], [], []
    for h in heads:
        ks = slice(h * RET_DK, (h + 1) * RET_DK)
        q = q_ref[:, ks]
        k = k_ref[:, ks]
        qr.append(q * cos + pltpu.roll(q, half, axis=1) * sin)
        kr.append((k * cos + pltpu.roll(k, half, axis=1) * sin) * (RET_DK ** -0.5))
        vb.append(v_ref[:, h * RET_DV:(h + 1) * RET_DV].astype(BF16))
    scores = [_dot_nt(qr[h].astype(BF16), kr[h].astype(BF16)) * w_ref[h] for h in heads]
    s_prev = [state_ref[h] for h in heads]
    o = [_dot(scores[h].astype(BF16), vb[h])
         + _dot((qr[h] * qd_ref[h]).astype(BF16), s_prev[h].astype(BF16)) for h in heads]
    for h in heads:
        state_ref[h] = (s_prev[h] * cd_ref[h][0:1, :]
                        + _dot_tn((kr[h] * kd_ref[h]).astype(BF16), vb[h]))
    for h in heads:
        vs = slice(h * RET_DV, (h + 1) * RET_DV)
        mu = jnp.mean(o[h], axis=-1, keepdims=True)
        oc = o[h] - mu
        var = jnp.mean(oc * oc, axis=-1, keepdims=True)
        y = oc * lax.rsqrt(var + 1e-5) * gn_ref[:, vs]
        o_ref[:, vs] = _silu(rg_ref[:, vs]) * y


def _retention(proj, cos2, sin2, tables, gn_g, batch, seq):
    wmat, qd, kd, cd = tables
    t = batch * seq
    nb = seq // RET_BLOCK
    L = RET_BLOCK
    qk_w = RET_HEADS * RET_DK
    v_w = RET_HEADS * RET_DV
    assert OFF_RQ % qk_w == 0 and OFF_RK % qk_w == 0 and OFF_RV % v_w == 0 and OFF_RG % v_w == 0

    def row(b, s):
        return b * nb + s

    def whole(x):
        return pl.BlockSpec(x.shape, lambda b, s: (0,) * x.ndim)

    gn2 = gn_g.reshape(1, v_w)
    return pl.pallas_call(
        _retention_kernel,
        grid=(batch, nb),
        in_specs=[
            pl.BlockSpec((L, qk_w), lambda b, s: (row(b, s), OFF_RQ // qk_w)),
            pl.BlockSpec((L, qk_w), lambda b, s: (row(b, s), OFF_RK // qk_w)),
            pl.BlockSpec((L, v_w), lambda b, s: (row(b, s), OFF_RV // v_w)),
            pl.BlockSpec((L, v_w), lambda b, s: (row(b, s), OFF_RG // v_w)),
            pl.BlockSpec((L, RET_DK), lambda b, s: (row(b, s), 0)),
            pl.BlockSpec((L, RET_DK), lambda b, s: (row(b, s), 0)),
            whole(wmat), whole(qd), whole(kd), whole(cd), whole(gn2),
        ],
        out_specs=pl.BlockSpec((L, v_w), lambda b, s: (row(b, s), 0)),
        out_shape=jax.ShapeDtypeStruct((t, v_w), F32),
        scratch_shapes=[pltpu.VMEM((RET_HEADS, RET_DK, RET_DV), F32)],
        compiler_params=_cparams(("arbitrary", "arbitrary")),
        name="retention",
    )(proj, proj, proj, proj, cos2, sin2, wmat, qd, kd, cd, gn2)


def _cumsum_rows(x):
    row = lax.broadcasted_iota(jnp.int32, x.shape, 0)
    s = 1
    while s < x.shape[0]:
        x = x + jnp.where(row >= s, pltpu.roll(x, s, axis=0), 0.0)
        s *= 2
    return x


def _softplus(x):
    return jnp.maximum(x, 0.0) + jnp.log1p(jnp.exp(-jnp.abs(x)))


def _gdn_kernel(qkv_ref, tail_ref, last_ref, convw_ref, alog_ref, dtb_ref,
                yr_ref, gng_ref, o_ref, state_ref, xext_ref):
    C = CHUNK
    HALO = 8
    NP = GDN_HEADS // 2
    DK, DV = GDN_DK, GDN_DV

    @pl.when(pl.program_id(1) == 0)
    def _():
        state_ref[...] = jnp.zeros_like(state_ref)
        xext_ref[0:HALO, :] = jnp.zeros((HALO, GDN_QKV), F32)

    xext_ref[HALO:HALO + C, :] = qkv_ref[...]

    sm = tail_ref[:, 0:LANE]
    gl = -jnp.exp(alog_ref[...]) * _softplus(sm + dtb_ref[...])
    cg = _cumsum_rows(gl)
    cg_t = cg.T
    bt = _sigmoid(sm)
    gng = gng_ref[...]

    row = lax.broadcasted_iota(jnp.int32, (C, 2 * C), 0)
    lane = lax.broadcasted_iota(jnp.int32, (C, 2 * C), 1)
    col = lane & (C - 1)
    lo = lane < C
    lo_t = lax.broadcasted_iota(jnp.int32, (2 * C, 2 * C), 1) < C
    causal = row >= col
    strict = row > col
    lvl_masks = []
    for b in range(C.bit_length() - 1):
        ib = lax.shift_right_logical(row, b)
        jb = lax.shift_right_logical(col, b)
        lvl_masks.append(((ib & 1) == 1) & (jb == ib - 1))

    def conv_act(col0):
        sl = slice(col0, col0 + LANE)
        y = convw_ref[CONV_K - 1:CONV_K, sl] * xext_ref[HALO:HALO + C, sl]
        for d in range(1, CONV_K):
            y = y + convw_ref[CONV_K - 1 - d:CONV_K - d, sl] * xext_ref[HALO - d:HALO - d + C, sl]
        return _silu(y)

    def block_diag(x, mask):
        return jnp.concatenate([jnp.where(mask, x, 0.0), jnp.where(mask, 0.0, x)], axis=0).astype(BF16)

    def block_diag_wide(xa, xb):
        za = jnp.zeros_like(xa)
        top = jnp.concatenate([xa, za], axis=1)
        bot = jnp.concatenate([za, xb], axis=1)
        return jnp.concatenate([top, bot], axis=0).astype(BF16)

    def tail_cols(off):
        lo_c = off - OFF_GA
        n_main = min(LANE, MAIN_WIDTH - off)
        if n_main == LANE:
            return tail_ref[:, lo_c:lo_c + LANE]
        return jnp.concatenate([tail_ref[:, lo_c:lo_c + n_main], last_ref[:, 0:LANE - n_main]], axis=1)

    def bcast_col(x, j):
        return jnp.broadcast_to(x[:, j:j + 1], (C, LANE))

    pairs = range(NP)
    qn, kn, vv = [], [], []
    for h in range(GDN_HEADS):
        q = conv_act(h * DK)
        k = conv_act(GDN_HEADS * DK + h * DK)
        vv.append(conv_act(2 * GDN_HEADS * DK + h * DV))
        qn.append(q * lax.rsqrt(jnp.sum(q * q, axis=-1, keepdims=True) + 1e-6) * (DK ** -0.5))
        kn.append(k * lax.rsqrt(jnp.sum(k * k, axis=-1, keepdims=True) + 1e-6))

    cgc, btc, ecg, kb = [], [], [], []
    for h in range(GDN_HEADS):
        cgc.append(bcast_col(cg, h))
        btc.append(bcast_col(bt, GDN_HEADS + h))
        ecg.append(jnp.exp(cgc[h]))
        kb.append(kn[h] * btc[h])

    decay, knt, m, qk = [], [], [], []
    for p in pairs:
        a, b = 2 * p, 2 * p + 1
        cgc_p = jnp.where(lo, cgc[a], cgc[b])
        cgr_p = jnp.concatenate([cg_t[a:a + 1, :], cg_t[b:b + 1, :]], axis=1)
        decay.append(jnp.where(causal, jnp.exp(jnp.where(causal, cgc_p - cgr_p, 0.0)), 0.0))
        knt.append(jnp.concatenate([kn[a], kn[b]], axis=0).T)
    for p in pairs:
        a, b = 2 * p, 2 * p + 1
        lhs = jnp.concatenate([jnp.concatenate([kb[a], kb[b]], axis=1),
                               jnp.concatenate([qn[a], qn[b]], axis=1)], axis=0)
        kq = _dot(lhs.astype(BF16), block_diag(knt[p], lo_t))
        m.append(jnp.where(strict, kq[:C] * decay[p], 0.0))
        qk.append(kq[C:] * decay[p])

    r = [jnp.where(lvl_masks[0], -m[p], 0.0) for p in pairs]
    for lm in lvl_masks[1:]:
        cm = [jnp.where(lm, m[p], 0.0) for p in pairs]
        y = [cm[p] + _dot(r[p].astype(BF16), block_diag(cm[p], lo)) for p in pairs]
        r = [r[p] - (y[p] + _dot(y[p].astype(BF16), block_diag(r[p], lo))) for p in pairs]

    u, w = [], []
    for p in pairs:
        a, b = 2 * p, 2 * p + 1
        xa = jnp.concatenate([vv[a] * btc[a], kb[a] * ecg[a]], axis=1)
        xb = jnp.concatenate([vv[b] * btc[b], kb[b] * ecg[b]], axis=1)
        rx = _dot(r[p].astype(BF16), block_diag_wide(xa, xb))
        u.append(jnp.concatenate([xa[:, :DV], xb[:, :DV]], axis=1) +
                 jnp.concatenate([rx[:, :DV], rx[:, DV + DK:2 * DV + DK]], axis=1))
        w.append(jnp.concatenate([xa[:, DV:], xb[:, DV:]], axis=1) +
                 jnp.concatenate([rx[:, DV:DV + DK], rx[:, 2 * DV + DK:]], axis=1))

    s_prev, wsqs, v_new = [], [], []
    for p in pairs:
        a, b = 2 * p, 2 * p + 1
        s_prev.append(state_ref[p])
        qe = jnp.concatenate([qn[a] * ecg[a], qn[b] * ecg[b]], axis=1)
        lhs = jnp.concatenate([w[p], qe], axis=0)
        wsqs.append(_dot(lhs.astype(BF16), block_diag_wide(s_prev[p][:, :DV], s_prev[p][:, DV:])))
        v_new.append(u[p] - wsqs[p][:C])
    for p in pairs:
        a, b = 2 * p, 2 * p + 1
        g_last_a = cgc[a][C - 1:C, :]
        g_last_b = cgc[b][C - 1:C, :]
        edr = jnp.exp(jnp.concatenate([g_last_a[:, :C] - cg_t[a:a + 1, :],
                                       g_last_b[:, :C] - cg_t[b:b + 1, :]], axis=1))
        lhs = jnp.concatenate([qk[p], knt[p] * edr], axis=0)
        res = _dot(lhs.astype(BF16), block_diag_wide(v_new[p][:, :DV], v_new[p][:, DV:]))
        eg = jnp.exp(jnp.concatenate([g_last_a, g_last_b], axis=1))
        state_ref[p] = s_prev[p] * eg + res[C:]
        o_pair = wsqs[p][C:] + res[:C]
        for i, h in enumerate((a, b)):
            hs = slice(h * DV, (h + 1) * DV)
            o = o_pair[:, i * DV:(i + 1) * DV]
            on = o * lax.rsqrt(jnp.mean(o * o, axis=-1, keepdims=True) + NORM_EPS) * gng
            yg = _silu(tail_cols(OFF_GZ + h * DV)) * on
            merged = (_sigmoid(tail_cols(OFF_GATE_A + h * DV)) * yr_ref[:, hs]
                      + _sigmoid(tail_cols(OFF_GATE_B + h * DV)) * yg)
            o_ref[:, hs] = merged.astype(o_ref.dtype)

    xext_ref[0:HALO, :] = xext_ref[C:C + HALO, :]


def _gdn_merge(proj, last, yr, conv_w, a_log, dt_bias, gdn_norm_g, batch, seq):
    t = batch * seq
    nc = seq // CHUNK
    C = CHUNK
    pad = LANE - GDN_HEADS
    alog_p = jnp.pad(a_log.astype(F32), (0, pad)).reshape(1, LANE)
    dtb_p = jnp.pad(dt_bias.astype(F32), (0, pad)).reshape(1, LANE)
    tail_w = MAIN_WIDTH - OFF_GA
    assert OFF_GQKV % GDN_QKV == 0 and OFF_GA % tail_w == 0

    def row(b, c):
        return b * nc + c

    return pl.pallas_call(
        _gdn_kernel,
        grid=(batch, nc),
        in_specs=[
            pl.BlockSpec((C, GDN_QKV), lambda b, c: (row(b, c), OFF_GQKV // GDN_QKV)),
            pl.BlockSpec((C, tail_w), lambda b, c: (row(b, c), OFF_GA // tail_w)),
            pl.BlockSpec((C, LANE), lambda b, c: (row(b, c), 0)),
            pl.BlockSpec((CONV_K, GDN_QKV), lambda b, c: (0, 0)),
            pl.BlockSpec((1, LANE), lambda b, c: (0, 0)),
            pl.BlockSpec((1, LANE), lambda b, c: (0, 0)),
            pl.BlockSpec((C, D_MODEL), lambda b, c: (row(b, c), 0)),
            pl.BlockSpec((1, GDN_DV), lambda b, c: (0, 0)),
        ],
        out_specs=pl.BlockSpec((C, D_MODEL), lambda b, c: (row(b, c), 0)),
        out_shape=jax.ShapeDtypeStruct((t, D_MODEL), BF16),
        scratch_shapes=[pltpu.VMEM((GDN_HEADS // 2, GDN_DK, 2 * GDN_DV), F32),
                        pltpu.VMEM((C + 8, GDN_QKV), F32)],
        compiler_params=_cparams(("arbitrary", "arbitrary")),
        name="gdn_merge",
    )(proj, proj, last, conv_w, alog_p, dtb_p, yr, gdn_norm_g.reshape(1, GDN_DV))


def _xattn_kernel(q_ref, k_ref, v_ref, o_ref):
    s = _dot_nt(q_ref[...], k_ref[...]) * (XATTN_DH ** -0.5)
    p = jnp.exp(s - jnp.max(s, axis=-1, keepdims=True))
    l = jnp.sum(p, axis=-1, keepdims=True)
    o = _dot(p.astype(BF16), v_ref[...]) / l
    o_ref[...] = o.astype(o_ref.dtype)


def _xattn(q, kv, batch, seq, mem_tokens, tq=512):
    t = batch * seq
    nq = seq // tq
    return pl.pallas_call(
        _xattn_kernel,
        grid=(batch, nq, XATTN_HEADS),
        in_specs=[
            pl.BlockSpec((tq, XATTN_DH), lambda b, i, h: (b * nq + i, h)),
            pl.BlockSpec((mem_tokens, XATTN_DH), lambda b, i, h: (b, h)),
            pl.BlockSpec((mem_tokens, XATTN_DH), lambda b, i, h: (b, XATTN_HEADS + h)),
        ],
        out_specs=pl.BlockSpec((tq, XATTN_DH), lambda b, i, h: (b * nq + i, h)),
        out_shape=jax.ShapeDtypeStruct((t, D_MODEL), BF16),
        compiler_params=_cparams(("parallel", "parallel", "parallel")),
        name="xattn",
    )(q, kv, kv)


def kernel(x, mem, positions, norm_mix_g, w_in, conv_w, gdn_a_log, gdn_dt_bias, ret_gn_g, gdn_norm_g, w_out, norm_x_g, norm_mem_g, w_xq, w_xkv, w_xo, norm_ffn_g, w_ff1, w_ff2, norm_final_g):
    batch, seq, d = x.shape
    mem_tokens = mem.shape[1]
    depth = w_in.shape[0]
    t = batch * seq
    x2 = x.reshape(t, d)
    mem2 = mem.reshape(batch * mem_tokens, d)

    cos2, sin2 = _rope_tables(positions)
    ret_tables = _retention_tables()

    w_last = jnp.pad(w_in[:, :, MAIN_WIDTH:], ((0, 0), (0, 0), (0, LANE - (IN_WIDTH - MAIN_WIDTH))))

    w_ff2_b = w_ff2.astype(BF16)

    for l in range(depth):
        h = _rmsnorm(x2, norm_mix_g[l], BF16)
        proj = _matmul(h, w_in, l, MAIN_WIDTH, F32, tm=1024, tn=1024)
        last = _matmul(h, w_last, l, LANE, F32, tm=1024, tn=LANE)
        yr = _retention(proj, cos2, sin2, ret_tables, ret_gn_g[l], batch, seq)
        merged = _gdn_merge(proj, last, yr, conv_w[l], gdn_a_log[l], gdn_dt_bias[l],
                            gdn_norm_g[l], batch, seq)
        x2 = _matmul(merged, w_out, l, d, F32, tm=1024, tn=1024, residual=x2)

        h = _rmsnorm(x2, norm_x_g[l], BF16)
        mem_n = _rmsnorm(mem2, norm_mem_g[l], BF16)
        q = _matmul(h, w_xq, l, d, BF16, tm=1024, tn=1024)
        kv = _matmul(mem_n, w_xkv, l, 2 * d, BF16, tm=batch * mem_tokens, tn=1024)
        o = _xattn(q, kv, batch, seq, mem_tokens)
        x2 = _matmul(o, w_xo, l, d, F32, tm=1024, tn=1024, residual=x2)

        h = _rmsnorm(x2, norm_ffn_g[l], BF16)
        hid = _matmul(h, w_ff1, l, D_FF, BF16, tm=1024, tn=1024, relu2=True)
        x2 = _matmul(hid, w_ff2_b, l, d, F32, tm=512, tn=512, residual=x2)

    out = _rmsnorm(x2, norm_final_g, x.dtype)
    return out.reshape(batch, seq, d)
```

```python
import functools

import numpy as np
import jax
import jax.numpy as jnp
from jax import lax
from jax.experimental import pallas as pl
from jax.experimental.pallas import tpu as pltpu

F32 = jnp.float32
BF16 = jnp.bfloat16

D_MODEL = 2048
CHUNK = 64
RET_HEADS = 8
RET_DK = D_MODEL // 16
RET_DV = D_MODEL // RET_HEADS
GDN_HEADS = 16
GDN_DK = D_MODEL // GDN_HEADS
GDN_DV = D_MODEL // GDN_HEADS
CONV_K = 4
XATTN_HEADS = 4
XATTN_DH = D_MODEL // XATTN_HEADS
D_FF = 4 * D_MODEL
ROPE_THETA = 10000.0
NORM_EPS = 1e-6
GDN_QKV = 3 * GDN_HEADS * GDN_DK
OFF_RQ = 0
OFF_RK = OFF_RQ + RET_HEADS * RET_DK
OFF_RV = OFF_RK + RET_HEADS * RET_DK
OFF_RG = OFF_RV + RET_HEADS * RET_DV
OFF_GQKV = OFF_RG + RET_HEADS * RET_DV
OFF_GA = OFF_GQKV + GDN_QKV
OFF_GB = OFF_GA + GDN_HEADS
OFF_GZ = OFF_GB + GDN_HEADS
OFF_GATE_A = OFF_GZ + GDN_HEADS * GDN_DV
OFF_GATE_B = OFF_GATE_A + D_MODEL
IN_WIDTH = OFF_GATE_B + D_MODEL

LANE = 128
PROJ_TN = 1024
MAIN_WIDTH = (IN_WIDTH // PROJ_TN) * PROJ_TN
RET_BLOCK = 256
GDN_STEP_CHUNKS = 2
VMEM_LIMIT = 56 * 1024 * 1024


def _cparams(sem):
    return pltpu.CompilerParams(dimension_semantics=sem, vmem_limit_bytes=VMEM_LIMIT)


def _dot(a, b):
    return jnp.dot(a, b, preferred_element_type=F32)


def _dot_nt(a, b):
    return lax.dot_general(a, b, (((1,), (1,)), ((), ())), preferred_element_type=F32)


def _dot_tn(a, b):
    return lax.dot_general(a, b, (((0,), (0,)), ((), ())), preferred_element_type=F32)


def _sigmoid(x):
    return 1.0 / (1.0 + jnp.exp(-x))


def _silu(x):
    return x * _sigmoid(x)


def _rmsnorm_kernel(x_ref, g_ref, o_ref):
    x = x_ref[...]
    ms = jnp.mean(x * x, axis=-1, keepdims=True)
    o_ref[...] = (x * lax.rsqrt(ms + NORM_EPS) * g_ref[...]).astype(o_ref.dtype)


def _rmsnorm(x, g, out_dtype, tm=512):
    m, d = x.shape
    tm = min(tm, m)
    return pl.pallas_call(
        _rmsnorm_kernel,
        grid=(m // tm,),
        in_specs=[pl.BlockSpec((tm, d), lambda i: (i, 0)),
                  pl.BlockSpec((1, d), lambda i: (0, 0))],
        out_specs=pl.BlockSpec((tm, d), lambda i: (i, 0)),
        out_shape=jax.ShapeDtypeStruct((m, d), out_dtype),
        compiler_params=_cparams(("parallel",)),
        name="rmsnorm",
    )(x, g.reshape(1, d))


def _mm_kernel(*refs, relu2, has_res, cast_w, n_valid):
    it = iter(refs)
    a_ref = next(it)
    w_ref = next(it)
    r_ref = next(it) if has_res else None
    o_ref = next(it)
    wbf_ref = next(it) if cast_w else w_ref

    if cast_w:
        @pl.when(pl.program_id(1) == 0)
        def _():
            w = w_ref[...]
            if n_valid is not None:
                lane = lax.broadcasted_iota(jnp.int32, w.shape, 1)
                w = jnp.where(lane < n_valid, w, 0.0)
            wbf_ref[...] = w.astype(BF16)

    acc = _dot(a_ref[...], wbf_ref[...])
    if relu2:
        acc = jnp.square(jnp.maximum(acc, 0.0))
    if has_res:
        acc = acc + r_ref[...]
    o_ref[...] = acc.astype(o_ref.dtype)


def _matmul(a, w, layer, n_out, out_dtype, tm, tn, relu2=False, residual=None, col0=0, n_valid=None):
    m, k = a.shape
    assert w.shape[1] == k and n_out % tn == 0 and m % tm == 0 and col0 % tn == 0
    cast_w = w.dtype != BF16 or n_valid is not None
    jb = col0 // tn
    in_specs = [pl.BlockSpec((tm, k), lambda j, i: (i, 0)),
                pl.BlockSpec((None, k, tn), lambda j, i: (layer, 0, jb + j))]
    args = [a, w]
    if residual is not None:
        in_specs.append(pl.BlockSpec((tm, tn), lambda j, i: (i, j)))
        args.append(residual)
    return pl.pallas_call(
        functools.partial(_mm_kernel, relu2=relu2, has_res=residual is not None, cast_w=cast_w,
                          n_valid=n_valid),
        grid=(n_out // tn, m // tm),
        in_specs=in_specs,
        out_specs=pl.BlockSpec((tm, tn), lambda j, i: (i, j)),
        out_shape=jax.ShapeDtypeStruct((m, n_out), out_dtype),
        scratch_shapes=[pltpu.VMEM((k, tn), BF16)] if cast_w else [],
        compiler_params=_cparams(("arbitrary", "arbitrary")),
        name="matmul",
    )(*args)


def _rope_kernel(pos_ref, invf_ref, sgn_ref, cos_ref, sin_ref):
    ang = pos_ref[...] * invf_ref[...]
    cos_ref[...] = jnp.cos(ang)
    sin_ref[...] = jnp.sin(ang) * sgn_ref[...]


def _rope_tables(positions):
    t = positions.size
    half = RET_DK // 2
    inv_freq = 1.0 / (ROPE_THETA ** (jnp.arange(0, RET_DK, 2, dtype=F32) / RET_DK))
    invf2 = jnp.concatenate([inv_freq, inv_freq]).reshape(1, RET_DK)
    sgn = jnp.concatenate([-jnp.ones((half,), F32), jnp.ones((half,), F32)]).reshape(1, RET_DK)
    pos = positions.astype(F32).reshape(t, 1)
    tm = 1024
    return pl.pallas_call(
        _rope_kernel,
        grid=(t // tm,),
        in_specs=[pl.BlockSpec((tm, 1), lambda i: (i, 0)),
                  pl.BlockSpec((1, RET_DK), lambda i: (0, 0)),
                  pl.BlockSpec((1, RET_DK), lambda i: (0, 0))],
        out_specs=[pl.BlockSpec((tm, RET_DK), lambda i: (i, 0))] * 2,
        out_shape=[jax.ShapeDtypeStruct((t, RET_DK), F32)] * 2,
        compiler_params=_cparams(("parallel",)),
        name="rope_tables",
    )(pos, invf2, sgn)


def _retention_tables():
    idx = np.arange(RET_BLOCK, dtype=np.float64)
    log_gamma = jnp.log1p(-jnp.exp2(-5.0 - jnp.arange(RET_HEADS, dtype=F32)))
    lg = log_gamma[:, None, None]
    dist = jnp.asarray(np.abs(idx[:, None] - idx[None, :]), F32)
    visible = jnp.asarray((idx[None, :] // CHUNK) <= (idx[:, None] // CHUNK))
    wmat = jnp.where(visible[None], jnp.exp(lg * dist[None]), 0.0)
    qd = jnp.exp(log_gamma[:, None] * jnp.asarray(idx + 1.0, F32))
    kd = jnp.exp(log_gamma[:, None] * jnp.asarray(RET_BLOCK - 1.0 - idx, F32))
    cd = jnp.exp(log_gamma * RET_BLOCK)
    qd = jnp.broadcast_to(qd[:, :, None], (RET_HEADS, RET_BLOCK, RET_DK))
    kd = jnp.broadcast_to(kd[:, :, None], (RET_HEADS, RET_BLOCK, RET_DK))
    cd = jnp.broadcast_to(cd[:, None, None], (RET_HEADS, 8, RET_DV))
    return wmat, qd, kd, cd


def _retention_kernel(q_ref, k_ref, v_ref, rg_ref, cos_ref, sin_ref, w_ref, qd_ref, kd_ref,
                      cd_ref, gn_ref, o_ref, state_ref):
    @pl.when(pl.program_id(1) == 0)
    def _():
        state_ref[...] = jnp.zeros_like(state_ref)

    cos = cos_ref[...]
    sin = sin_ref[...]
    half = RET_DK // 2
    heads = range(RET_HEADS)
    qr, kr, vb = [], [], []
    for h in heads:
        ks = slice(h * RET_DK, (h + 1) * RET_DK)
        q = q_ref[:, ks]
        k = k_ref[:, ks]
        qr.append(q * cos + pltpu.roll(q, half, axis=1) * sin)
        kr.append((k * cos + pltpu.roll(k, half, axis=1) * sin) * (RET_DK ** -0.5))
        vb.append(v_ref[:, h * RET_DV:(h + 1) * RET_DV].astype(BF16))
    scores = [_dot_nt(qr[h].astype(BF16), kr[h].astype(BF16)) * w_ref[h] for h in heads]
    s_prev = [state_ref[h] for h in heads]
    o = [_dot(scores[h].astype(BF16), vb[h])
         + _dot((qr[h] * qd_ref[h]).astype(BF16), s_prev[h].astype(BF16)) for h in heads]
    for h in heads:
        state_ref[h] = (s_prev[h] * cd_ref[h][0:1, :]
                        + _dot_tn((kr[h] * kd_ref[h]).astype(BF16), vb[h]))
    for h in heads:
        vs = slice(h * RET_DV, (h + 1) * RET_DV)
        mu = jnp.mean(o[h], axis=-1, keepdims=True)
        oc = o[h] - mu
        var = jnp.mean(oc * oc, axis=-1, keepdims=True)
        y = oc * lax.rsqrt(var + 1e-5) * gn_ref[:, vs]
        o_ref[:, vs] = _silu(rg_ref[:, vs]) * y


def _retention(proj, cos2, sin2, tables, gn_g, batch, seq):
    wmat, qd, kd, cd = tables
    t = batch * seq
    nb = seq // RET_BLOCK
    L = RET_BLOCK
    qk_w = RET_HEADS * RET_DK
    v_w = RET_HEADS * RET_DV
    assert OFF_RQ % qk_w == 0 and OFF_RK % qk_w == 0 and OFF_RV % v_w == 0 and OFF_RG % v_w == 0

    def row(b, s):
        return b * nb + s

    def whole(x):
        return pl.BlockSpec(x.shape, lambda b, s: (0,) * x.ndim)

    gn2 = gn_g.reshape(1, v_w)
    return pl.pallas_call(
        _retention_kernel,
        grid=(batch, nb),
        in_specs=[
            pl.BlockSpec((L, qk_w), lambda b, s: (row(b, s), OFF_RQ // qk_w)),
            pl.BlockSpec((L, qk_w), lambda b, s: (row(b, s), OFF_RK // qk_w)),
            pl.BlockSpec((L, v_w), lambda b, s: (row(b, s), OFF_RV // v_w)),
            pl.BlockSpec((L, v_w), lambda b, s: (row(b, s), OFF_RG // v_w)),
            pl.BlockSpec((L, RET_DK), lambda b, s: (row(b, s), 0)),
            pl.BlockSpec((L, RET_DK), lambda b, s: (row(b, s), 0)),
            whole(wmat), whole(qd), whole(kd), whole(cd), whole(gn2),
        ],
        out_specs=pl.BlockSpec((L, v_w), lambda b, s: (row(b, s), 0)),
        out_shape=jax.ShapeDtypeStruct((t, v_w), F32),
        scratch_shapes=[pltpu.VMEM((RET_HEADS, RET_DK, RET_DV), F32)],
        compiler_params=_cparams(("arbitrary", "arbitrary")),
        name="retention",
    )(proj, proj, proj, proj, cos2, sin2, wmat, qd, kd, cd, gn2)


def _cumsum_rows(x):
    row = lax.broadcasted_iota(jnp.int32, x.shape, 0)
    s = 1
    while s < x.shape[0]:
        x = x + jnp.where(row >= s, pltpu.roll(x, s, axis=0), 0.0)
        s *= 2
    return x


def _softplus(x):
    return jnp.maximum(x, 0.0) + jnp.log1p(jnp.exp(-jnp.abs(x)))


def _gdn_kernel(qkv_ref, tail_ref, last_ref, convw_ref, alog_ref, dtb_ref,
                yr_ref, gng_ref, o_ref, state_ref, xext_ref):
    C = CHUNK
    HALO = 8
    NB = GDN_STEP_CHUNKS
    NP = GDN_HEADS // 2
    DK, DV = GDN_DK, GDN_DV

    @pl.when(pl.program_id(1) == 0)
    def _():
        state_ref[...] = jnp.zeros_like(state_ref)
        xext_ref[0:HALO, :] = jnp.zeros((HALO, GDN_QKV), F32)

    xext_ref[HALO:HALO + NB * C, :] = qkv_ref[...]

    gng = gng_ref[...]
    row = lax.broadcasted_iota(jnp.int32, (C, 2 * C), 0)
    lane = lax.broadcasted_iota(jnp.int32, (C, 2 * C), 1)
    col = lane & (C - 1)
    lo = lane < C
    lo_t = lax.broadcasted_iota(jnp.int32, (2 * C, 2 * C), 1) < C
    causal = row >= col
    strict = row > col
    lvl_masks = []
    for b in range(C.bit_length() - 1):
        ib = lax.shift_right_logical(row, b)
        jb = lax.shift_right_logical(col, b)
        lvl_masks.append(((ib & 1) == 1) & (jb == ib - 1))

    def conv_act(ci, col0):
        sl = slice(col0, col0 + LANE)
        r0 = HALO + ci * C
        y = convw_ref[CONV_K - 1:CONV_K, sl] * xext_ref[r0:r0 + C, sl]
        for d in range(1, CONV_K):
            y = y + convw_ref[CONV_K - 1 - d:CONV_K - d, sl] * xext_ref[r0 - d:r0 - d + C, sl]
        return _silu(y)

    def block_diag(x, mask):
        return jnp.concatenate([jnp.where(mask, x, 0.0), jnp.where(mask, 0.0, x)], axis=0).astype(BF16)

    def block_diag_wide(xa, xb):
        za = jnp.zeros_like(xa)
        top = jnp.concatenate([xa, za], axis=1)
        bot = jnp.concatenate([za, xb], axis=1)
        return jnp.concatenate([top, bot], axis=0).astype(BF16)

    def tail_cols(ci, off):
        rs = slice(ci * C, (ci + 1) * C)
        lo_c = off - OFF_GA
        n_main = min(LANE, MAIN_WIDTH - off)
        if n_main == LANE:
            return tail_ref[rs, lo_c:lo_c + LANE]
        return jnp.concatenate([tail_ref[rs, lo_c:lo_c + n_main], last_ref[rs, 0:LANE - n_main]], axis=1)

    def bcast_col(x, j):
        return jnp.broadcast_to(x[:, j:j + 1], (C, LANE))

    chunks = range(NB)
    heads = range(GDN_HEADS)
    items = [(ci, p) for ci in chunks for p in range(NP)]

    cg, cg_t, bt = [], [], []
    for ci in chunks:
        sm = tail_ref[ci * C:(ci + 1) * C, 0:LANE]
        gl = -jnp.exp(alog_ref[...]) * _softplus(sm + dtb_ref[...])
        cg.append(_cumsum_rows(gl))
        cg_t.append(cg[ci].T)
        bt.append(_sigmoid(sm))

    qn, kn, vv, cgc, btc, ecg, kb = {}, {}, {}, {}, {}, {}, {}
    for ci in chunks:
        for h in heads:
            q = conv_act(ci, h * DK)
            k = conv_act(ci, GDN_HEADS * DK + h * DK)
            vv[ci, h] = conv_act(ci, 2 * GDN_HEADS * DK + h * DV)
            qn[ci, h] = q * lax.rsqrt(jnp.sum(q * q, axis=-1, keepdims=True) + 1e-6) * (DK ** -0.5)
            kn[ci, h] = k * lax.rsqrt(jnp.sum(k * k, axis=-1, keepdims=True) + 1e-6)
    for ci in chunks:
        for h in heads:
            cgc[ci, h] = bcast_col(cg[ci], h)
            btc[ci, h] = bcast_col(bt[ci], GDN_HEADS + h)
            ecg[ci, h] = jnp.exp(cgc[ci, h])
            kb[ci, h] = kn[ci, h] * btc[ci, h]

    decay, knt, m, qk = {}, {}, {}, {}
    for it in items:
        ci, p = it
        a, b = (ci, 2 * p), (ci, 2 * p + 1)
        cgc_p = jnp.where(lo, cgc[a], cgc[b])
        cgr_p = jnp.concatenate([cg_t[ci][2 * p:2 * p + 1, :], cg_t[ci][2 * p + 1:2 * p + 2, :]], axis=1)
        decay[it] = jnp.where(causal, jnp.exp(jnp.where(causal, cgc_p - cgr_p, 0.0)), 0.0)
        knt[it] = jnp.concatenate([kn[a], kn[b]], axis=0).T
    for it in items:
        ci, p = it
        a, b = (ci, 2 * p), (ci, 2 * p + 1)
        lhs = jnp.concatenate([jnp.concatenate([kb[a], kb[b]], axis=1),
                               jnp.concatenate([qn[a], qn[b]], axis=1)], axis=0)
        kq = _dot(lhs.astype(BF16), block_diag(knt[it], lo_t))
        m[it] = jnp.where(strict, kq[:C] * decay[it], 0.0)
        qk[it] = kq[C:] * decay[it]

    r = {it: jnp.where(lvl_masks[0], -m[it], 0.0) for it in items}
    for lm in lvl_masks[1:]:
        cm = {it: jnp.where(lm, m[it], 0.0) for it in items}
        y = {it: cm[it] + _dot(r[it].astype(BF16), block_diag(cm[it], lo)) for it in items}
        r = {it: r[it] - (y[it] + _dot(y[it].astype(BF16), block_diag(r[it], lo))) for it in items}

    u, w = {}, {}
    for it in items:
        ci, p = it
        a, b = (ci, 2 * p), (ci, 2 * p + 1)
        xa = jnp.concatenate([vv[a] * btc[a], kb[a] * ecg[a]], axis=1)
        xb = jnp.concatenate([vv[b] * btc[b], kb[b] * ecg[b]], axis=1)
        rx = _dot(r[it].astype(BF16), block_diag_wide(xa, xb))
        u[it] = (jnp.concatenate([xa[:, :DV], xb[:, :DV]], axis=1) +
                 jnp.concatenate([rx[:, :DV], rx[:, DV + DK:2 * DV + DK]], axis=1))
        w[it] = (jnp.concatenate([xa[:, DV:], xb[:, DV:]], axis=1) +
                 jnp.concatenate([rx[:, DV:DV + DK], rx[:, 2 * DV + DK:]], axis=1))

    state = [state_ref[p] for p in range(NP)]
    for ci in chunks:
        wsqs, v_new = {}, {}
        for p in range(NP):
            it = (ci, p)
            a, b = (ci, 2 * p), (ci, 2 * p + 1)
            qe = jnp.concatenate([qn[a] * ecg[a], qn[b] * ecg[b]], axis=1)
            lhs = jnp.concatenate([w[it], qe], axis=0)
            wsqs[p] = _dot(lhs.astype(BF16), block_diag_wide(state[p][:, :DV], state[p][:, DV:]))
            v_new[p] = u[it] - wsqs[p][:C]
        for p in range(NP):
            it = (ci, p)
            a, b = (ci, 2 * p), (ci, 2 * p + 1)
            g_last_a = cgc[a][C - 1:C, :]
            g_last_b = cgc[b][C - 1:C, :]
            edr = jnp.exp(jnp.concatenate([g_last_a[:, :C] - cg_t[ci][2 * p:2 * p + 1, :],
                                           g_last_b[:, :C] - cg_t[ci][2 * p + 1:2 * p + 2, :]], axis=1))
            lhs = jnp.concatenate([qk[it], knt[it] * edr], axis=0)
            res = _dot(lhs.astype(BF16), block_diag_wide(v_new[p][:, :DV], v_new[p][:, DV:]))
            eg = jnp.exp(jnp.concatenate([g_last_a, g_last_b], axis=1))
            state[p] = state[p] * eg + res[C:]
            o_pair = wsqs[p][C:] + res[:C]
            for i, h in enumerate((2 * p, 2 * p + 1)):
                hs = slice(h * DV, (h + 1) * DV)
                rs = slice(ci * C, (ci + 1) * C)
                o = o_pair[:, i * DV:(i + 1) * DV]
                on = o * lax.rsqrt(jnp.mean(o * o, axis=-1, keepdims=True) + NORM_EPS) * gng
                yg = _silu(tail_cols(ci, OFF_GZ + h * DV)) * on
                merged = (_sigmoid(tail_cols(ci, OFF_GATE_A + h * DV)) * yr_ref[rs, hs]
                          + _sigmoid(tail_cols(ci, OFF_GATE_B + h * DV)) * yg)
                o_ref[rs, hs] = merged.astype(o_ref.dtype)
    for p in range(NP):
        state_ref[p] = state[p]

    xext_ref[0:HALO, :] = xext_ref[NB * C:NB * C + HALO, :]


def _gdn_merge(proj, last, yr, conv_w, a_log, dt_bias, gdn_norm_g, batch, seq):
    t = batch * seq
    R = GDN_STEP_CHUNKS * CHUNK
    ns = seq // R
    pad = LANE - GDN_HEADS
    alog_p = jnp.pad(a_log.astype(F32), (0, pad)).reshape(1, LANE)
    dtb_p = jnp.pad(dt_bias.astype(F32), (0, pad)).reshape(1, LANE)
    tail_w = MAIN_WIDTH - OFF_GA
    assert OFF_GQKV % GDN_QKV == 0 and OFF_GA % tail_w == 0 and seq % R == 0

    def row(b, c):
        return b * ns + c

    return pl.pallas_call(
        _gdn_kernel,
        grid=(batch, ns),
        in_specs=[
            pl.BlockSpec((R, GDN_QKV), lambda b, c: (row(b, c), OFF_GQKV // GDN_QKV)),
            pl.BlockSpec((R, tail_w), lambda b, c: (row(b, c), OFF_GA // tail_w)),
            pl.BlockSpec((R, LANE), lambda b, c: (row(b, c), 0)),
            pl.BlockSpec((CONV_K, GDN_QKV), lambda b, c: (0, 0)),
            pl.BlockSpec((1, LANE), lambda b, c: (0, 0)),
            pl.BlockSpec((1, LANE), lambda b, c: (0, 0)),
            pl.BlockSpec((R, D_MODEL), lambda b, c: (row(b, c), 0)),
            pl.BlockSpec((1, GDN_DV), lambda b, c: (0, 0)),
        ],
        out_specs=pl.BlockSpec((R, D_MODEL), lambda b, c: (row(b, c), 0)),
        out_shape=jax.ShapeDtypeStruct((t, D_MODEL), BF16),
        scratch_shapes=[pltpu.VMEM((GDN_HEADS // 2, GDN_DK, 2 * GDN_DV), F32),
                        pltpu.VMEM((R + 8, GDN_QKV), F32)],
        compiler_params=_cparams(("arbitrary", "arbitrary")),
        name="gdn_merge",
    )(proj, proj, last, conv_w, alog_p, dtb_p, yr, gdn_norm_g.reshape(1, GDN_DV))


def _xa_fold_q_kernel(wq_ref, k_ref, o_ref):
    a = _dot_nt(wq_ref[...].astype(BF16), k_ref[...]) * (XATTN_DH ** -0.5)
    o_ref[...] = a.astype(o_ref.dtype)


def _xa_fold_o_kernel(v_ref, wo_ref, o_ref):
    o_ref[...] = _dot(v_ref[...], wo_ref[...].astype(BF16)).astype(o_ref.dtype)


def _xa_main_kernel(h_ref, a_ref, b_ref, r_ref, o_ref, *, mem_tokens):
    s = _dot(h_ref[...], a_ref[...])
    ps = []
    for hh in range(XATTN_HEADS):
        seg = s[:, hh * mem_tokens:(hh + 1) * mem_tokens]
        p = jnp.exp(seg - jnp.max(seg, axis=-1, keepdims=True))
        ps.append((p / jnp.sum(p, axis=-1, keepdims=True)).astype(BF16))
    o_ref[...] = _dot(jnp.concatenate(ps, axis=1), b_ref[...]) + r_ref[...]


def _xattn_residual(h, kv, w_xq, w_xo, layer, x2, batch, seq, mem_tokens, tq=512):
    t, d = h.shape
    hm = XATTN_HEADS * mem_tokens
    nq = seq // tq
    grid_bh = (batch, XATTN_HEADS)
    fold_q = pl.pallas_call(
        _xa_fold_q_kernel,
        grid=grid_bh,
        in_specs=[pl.BlockSpec((None, d, XATTN_DH), lambda b, hh: (layer, 0, hh)),
                  pl.BlockSpec((mem_tokens, XATTN_DH), lambda b, hh: (b, hh))],
        out_specs=pl.BlockSpec((None, d, mem_tokens), lambda b, hh: (b, 0, hh)),
        out_shape=jax.ShapeDtypeStruct((batch, d, hm), BF16),
        compiler_params=_cparams(("parallel", "parallel")),
        name="xattn_fold_q",
    )(w_xq, kv)
    fold_o = pl.pallas_call(
        _xa_fold_o_kernel,
        grid=grid_bh,
        in_specs=[pl.BlockSpec((mem_tokens, XATTN_DH), lambda b, hh: (b, XATTN_HEADS + hh)),
                  pl.BlockSpec((None, XATTN_DH, d), lambda b, hh: (layer, hh, 0))],
        out_specs=pl.BlockSpec((None, mem_tokens, d), lambda b, hh: (b, hh, 0)),
        out_shape=jax.ShapeDtypeStruct((batch, hm, d), BF16),
        compiler_params=_cparams(("parallel", "parallel")),
        name="xattn_fold_o",
    )(kv, w_xo)
    return pl.pallas_call(
        functools.partial(_xa_main_kernel, mem_tokens=mem_tokens),
        grid=(batch, nq),
        in_specs=[pl.BlockSpec((tq, d), lambda b, i: (b * nq + i, 0)),
                  pl.BlockSpec((None, d, hm), lambda b, i: (b, 0, 0)),
                  pl.BlockSpec((None, hm, d), lambda b, i: (b, 0, 0)),
                  pl.BlockSpec((tq, d), lambda b, i: (b * nq + i, 0))],
        out_specs=pl.BlockSpec((tq, d), lambda b, i: (b * nq + i, 0)),
        out_shape=jax.ShapeDtypeStruct((t, d), F32),
        compiler_params=_cparams(("parallel", "parallel")),
        name="xattn",
    )(h, fold_q, fold_o, x2)


def kernel(x, mem, positions, norm_mix_g, w_in, conv_w, gdn_a_log, gdn_dt_bias, ret_gn_g, gdn_norm_g, w_out, norm_x_g, norm_mem_g, w_xq, w_xkv, w_xo, norm_ffn_g, w_ff1, w_ff2, norm_final_g):
    batch, seq, d = x.shape
    mem_tokens = mem.shape[1]
    depth = w_in.shape[0]
    t = batch * seq
    x2 = x.reshape(t, d)
    mem2 = mem.reshape(batch * mem_tokens, d)

    cos2, sin2 = _rope_tables(positions)
    ret_tables = _retention_tables()
    w_ff2_b = w_ff2.astype(BF16)

    for l in range(depth):
        h = _rmsnorm(x2, norm_mix_g[l], BF16)
        proj = _matmul(h, w_in, l, MAIN_WIDTH, F32, tm=1024, tn=PROJ_TN)
        last = _matmul(h, w_in, l, LANE, F32, tm=1024, tn=LANE, col0=MAIN_WIDTH,
                       n_valid=IN_WIDTH - MAIN_WIDTH)
        yr = _retention(proj, cos2, sin2, ret_tables, ret_gn_g[l], batch, seq)
        merged = _gdn_merge(proj, last, yr, conv_w[l], gdn_a_log[l], gdn_dt_bias[l],
                            gdn_norm_g[l], batch, seq)
        x2 = _matmul(merged, w_out, l, d, F32, tm=1024, tn=1024, residual=x2)

        h = _rmsnorm(x2, norm_x_g[l], BF16)
        mem_n = _rmsnorm(mem2, norm_mem_g[l], BF16)
        kv = _matmul(mem_n, w_xkv, l, 2 * d, BF16, tm=batch * mem_tokens, tn=1024)
        x2 = _xattn_residual(h, kv, w_xq, w_xo, l, x2, batch, seq, mem_tokens)

        h = _rmsnorm(x2, norm_ffn_g[l], BF16)
        hid = _matmul(h, w_ff1, l, D_FF, BF16, tm=1024, tn=1024, relu2=True)
        x2 = _matmul(hid, w_ff2_b, l, d, F32, tm=512, tn=512, residual=x2)

    out = _rmsnorm(x2, norm_final_g, x.dtype)
    return out.reshape(batch, seq, d)
```

```python
import functools

import numpy as np
import jax
import jax.numpy as jnp
from jax import lax
from jax.experimental import pallas as pl
from jax.experimental.pallas import tpu as pltpu

F32 = jnp.float32
BF16 = jnp.bfloat16

D_MODEL = 2048
CHUNK = 64
RET_HEADS = 8
RET_DK = D_MODEL // 16
RET_DV = D_MODEL // RET_HEADS
GDN_HEADS = 16
GDN_DK = D_MODEL // GDN_HEADS
GDN_DV = D_MODEL // GDN_HEADS
CONV_K = 4
XATTN_HEADS = 4
XATTN_DH = D_MODEL // XATTN_HEADS
D_FF = 4 * D_MODEL
ROPE_THETA = 10000.0
NORM_EPS = 1e-6
GDN_QKV = 3 * GDN_HEADS * GDN_DK
OFF_RQ = 0
OFF_RK = OFF_RQ + RET_HEADS * RET_DK
OFF_RV = OFF_RK + RET_HEADS * RET_DK
OFF_RG = OFF_RV + RET_HEADS * RET_DV
OFF_GQKV = OFF_RG + RET_HEADS * RET_DV
OFF_GA = OFF_GQKV + GDN_QKV
OFF_GB = OFF_GA + GDN_HEADS
OFF_GZ = OFF_GB + GDN_HEADS
OFF_GATE_A = OFF_GZ + GDN_HEADS * GDN_DV
OFF_GATE_B = OFF_GATE_A + D_MODEL
IN_WIDTH = OFF_GATE_B + D_MODEL

LANE = 128
PROJ_TN = 1024
MAIN_WIDTH = (IN_WIDTH // PROJ_TN) * PROJ_TN
RET_BLOCK = 256
GDN_STEP_CHUNKS = 2
VMEM_LIMIT = 56 * 1024 * 1024


def _cparams(sem):
    return pltpu.CompilerParams(dimension_semantics=sem, vmem_limit_bytes=VMEM_LIMIT)


def _dot(a, b):
    return jnp.dot(a, b, preferred_element_type=F32)


def _dot_nt(a, b):
    return lax.dot_general(a, b, (((1,), (1,)), ((), ())), preferred_element_type=F32)


def _dot_tn(a, b):
    return lax.dot_general(a, b, (((0,), (0,)), ((), ())), preferred_element_type=F32)


def _sigmoid(x):
    return 1.0 / (1.0 + jnp.exp(-x))


def _silu(x):
    return x * _sigmoid(x)


def _rmsnorm_kernel(x_ref, g_ref, o_ref):
    x = x_ref[...]
    ms = jnp.mean(x * x, axis=-1, keepdims=True)
    o_ref[...] = (x * lax.rsqrt(ms + NORM_EPS) * g_ref[...]).astype(o_ref.dtype)


def _rmsnorm(x, g, out_dtype, tm=512):
    m, d = x.shape
    tm = min(tm, m)
    return pl.pallas_call(
        _rmsnorm_kernel,
        grid=(m // tm,),
        in_specs=[pl.BlockSpec((tm, d), lambda i: (i, 0)),
                  pl.BlockSpec((1, d), lambda i: (0, 0))],
        out_specs=pl.BlockSpec((tm, d), lambda i: (i, 0)),
        out_shape=jax.ShapeDtypeStruct((m, d), out_dtype),
        compiler_params=_cparams(("parallel",)),
        name="rmsnorm",
    )(x, g.reshape(1, d))


def _mm_kernel(*refs, relu2, has_res, cast_w, n_valid, w_is_nk):
    it = iter(refs)
    a_ref = next(it)
    w_ref = next(it)
    r_ref = next(it) if has_res else None
    o_ref = next(it)
    wbf_ref = next(it) if cast_w else w_ref

    if cast_w:
        @pl.when(pl.program_id(1) == 0)
        def _():
            w = w_ref[...]
            if n_valid is not None:
                idx = lax.broadcasted_iota(jnp.int32, w.shape, 0 if w_is_nk else 1)
                w = jnp.where(idx < n_valid, w, 0.0)
            if w_is_nk:
                w = w.T
            wbf_ref[...] = w.astype(BF16)

    acc = _dot(a_ref[...], wbf_ref[...])
    if relu2:
        acc = jnp.square(jnp.maximum(acc, 0.0))
    if has_res:
        acc = acc + r_ref[...]
    o_ref[...] = acc.astype(o_ref.dtype)


def _matmul(a, w, layer, n_out, out_dtype, tm, tn, relu2=False, residual=None, col0=0, n_valid=None,
            w_is_nk=False):
    m, k = a.shape
    assert w.shape[2 if w_is_nk else 1] == k and n_out % tn == 0 and m % tm == 0 and col0 % tn == 0
    cast_w = w.dtype != BF16 or n_valid is not None or w_is_nk
    jb = col0 // tn
    if w_is_nk:
        w_spec = pl.BlockSpec((None, tn, k), lambda j, i: (layer, jb + j, 0))
    else:
        w_spec = pl.BlockSpec((None, k, tn), lambda j, i: (layer, 0, jb + j))
    in_specs = [pl.BlockSpec((tm, k), lambda j, i: (i, 0)), w_spec]
    args = [a, w]
    if residual is not None:
        in_specs.append(pl.BlockSpec((tm, tn), lambda j, i: (i, j)))
        args.append(residual)
    return pl.pallas_call(
        functools.partial(_mm_kernel, relu2=relu2, has_res=residual is not None, cast_w=cast_w,
                          n_valid=n_valid, w_is_nk=w_is_nk),
        grid=(n_out // tn, m // tm),
        in_specs=in_specs,
        out_specs=pl.BlockSpec((tm, tn), lambda j, i: (i, j)),
        out_shape=jax.ShapeDtypeStruct((m, n_out), out_dtype),
        scratch_shapes=[pltpu.VMEM((k, tn), BF16)] if cast_w else [],
        compiler_params=_cparams(("arbitrary", "arbitrary")),
        name="matmul",
    )(*args)


def _rope_kernel(pos_ref, invf_ref, sgn_ref, cos_ref, sin_ref):
    ang = pos_ref[...] * invf_ref[...]
    cos_ref[...] = jnp.cos(ang)
    sin_ref[...] = jnp.sin(ang) * sgn_ref[...]


def _rope_tables(positions):
    t = positions.size
    half = RET_DK // 2
    inv_freq = 1.0 / (ROPE_THETA ** (jnp.arange(0, RET_DK, 2, dtype=F32) / RET_DK))
    invf2 = jnp.concatenate([inv_freq, inv_freq]).reshape(1, RET_DK)
    sgn = jnp.concatenate([-jnp.ones((half,), F32), jnp.ones((half,), F32)]).reshape(1, RET_DK)
    pos = positions.astype(F32).reshape(t, 1)
    tm = 1024
    return pl.pallas_call(
        _rope_kernel,
        grid=(t // tm,),
        in_specs=[pl.BlockSpec((tm, 1), lambda i: (i, 0)),
                  pl.BlockSpec((1, RET_DK), lambda i: (0, 0)),
                  pl.BlockSpec((1, RET_DK), lambda i: (0, 0))],
        out_specs=[pl.BlockSpec((tm, RET_DK), lambda i: (i, 0))] * 2,
        out_shape=[jax.ShapeDtypeStruct((t, RET_DK), F32)] * 2,
        compiler_params=_cparams(("parallel",)),
        name="rope_tables",
    )(pos, invf2, sgn)


def _retention_tables():
    idx = np.arange(RET_BLOCK, dtype=np.float64)
    log_gamma = jnp.log1p(-jnp.exp2(-5.0 - jnp.arange(RET_HEADS, dtype=F32)))
    lg = log_gamma[:, None, None]
    dist = jnp.asarray(np.abs(idx[:, None] - idx[None, :]), F32)
    visible = jnp.asarray((idx[None, :] // CHUNK) <= (idx[:, None] // CHUNK))
    wmat = jnp.where(visible[None], jnp.exp(lg * dist[None]), 0.0)
    qd = jnp.exp(log_gamma[:, None] * jnp.asarray(idx + 1.0, F32))
    kd = jnp.exp(log_gamma[:, None] * jnp.asarray(RET_BLOCK - 1.0 - idx, F32))
    cd = jnp.exp(log_gamma * RET_BLOCK)
    qd = jnp.broadcast_to(qd[:, :, None], (RET_HEADS, RET_BLOCK, RET_DK))
    kd = jnp.broadcast_to(kd[:, :, None], (RET_HEADS, RET_BLOCK, RET_DK))
    cd = jnp.broadcast_to(cd[:, None, None], (RET_HEADS, 8, RET_DV))
    return wmat, qd, kd, cd


def _retention_kernel(q_ref, k_ref, v_ref, rg_ref, cos_ref, sin_ref, w_ref, qd_ref, kd_ref,
                      cd_ref, gn_ref, o_ref, state_ref):
    @pl.when(pl.program_id(1) == 0)
    def _():
        state_ref[...] = jnp.zeros_like(state_ref)

    cos = cos_ref[...]
    sin = sin_ref[...]
    half = RET_DK // 2
    heads = range(RET_HEADS)
    qr, kr, vb = [], [], []
    for h in heads:
        ks = slice(h * RET_DK, (h + 1) * RET_DK)
        q = q_ref[:, ks]
        k = k_ref[:, ks]
        qr.append(q * cos + pltpu.roll(q, half, axis=1) * sin)
        kr.append((k * cos + pltpu.roll(k, half, axis=1) * sin) * (RET_DK ** -0.5))
        vb.append(v_ref[:, h * RET_DV:(h + 1) * RET_DV].astype(BF16))
    scores = [_dot_nt(qr[h].astype(BF16), kr[h].astype(BF16)) * w_ref[h] for h in heads]
    s_prev = [state_ref[h] for h in heads]
    o = [_dot(scores[h].astype(BF16), vb[h])
         + _dot((qr[h] * qd_ref[h]).astype(BF16), s_prev[h].astype(BF16)) for h in heads]
    for h in heads:
        state_ref[h] = (s_prev[h] * cd_ref[h][0:1, :]
                        + _dot_tn((kr[h] * kd_ref[h]).astype(BF16), vb[h]))
    for h in heads:
        vs = slice(h * RET_DV, (h + 1) * RET_DV)
        mu = jnp.mean(o[h], axis=-1, keepdims=True)
        oc = o[h] - mu
        var = jnp.mean(oc * oc, axis=-1, keepdims=True)
        y = oc * lax.rsqrt(var + 1e-5) * gn_ref[:, vs]
        o_ref[:, vs] = _silu(rg_ref[:, vs]) * y


def _retention(proj, cos2, sin2, tables, gn_g, batch, seq):
    wmat, qd, kd, cd = tables
    t = batch * seq
    nb = seq // RET_BLOCK
    L = RET_BLOCK
    qk_w = RET_HEADS * RET_DK
    v_w = RET_HEADS * RET_DV
    assert OFF_RQ % qk_w == 0 and OFF_RK % qk_w == 0 and OFF_RV % v_w == 0 and OFF_RG % v_w == 0

    def row(b, s):
        return b * nb + s

    def whole(x):
        return pl.BlockSpec(x.shape, lambda b, s: (0,) * x.ndim)

    gn2 = gn_g.reshape(1, v_w)
    return pl.pallas_call(
        _retention_kernel,
        grid=(batch, nb),
        in_specs=[
            pl.BlockSpec((L, qk_w), lambda b, s: (row(b, s), OFF_RQ // qk_w)),
            pl.BlockSpec((L, qk_w), lambda b, s: (row(b, s), OFF_RK // qk_w)),
            pl.BlockSpec((L, v_w), lambda b, s: (row(b, s), OFF_RV // v_w)),
            pl.BlockSpec((L, v_w), lambda b, s: (row(b, s), OFF_RG // v_w)),
            pl.BlockSpec((L, RET_DK), lambda b, s: (row(b, s), 0)),
            pl.BlockSpec((L, RET_DK), lambda b, s: (row(b, s), 0)),
            whole(wmat), whole(qd), whole(kd), whole(cd), whole(gn2),
        ],
        out_specs=pl.BlockSpec((L, v_w), lambda b, s: (row(b, s), 0)),
        out_shape=jax.ShapeDtypeStruct((t, v_w), F32),
        scratch_shapes=[pltpu.VMEM((RET_HEADS, RET_DK, RET_DV), F32)],
        compiler_params=_cparams(("arbitrary", "arbitrary")),
        name="retention",
    )(proj, proj, proj, proj, cos2, sin2, wmat, qd, kd, cd, gn2)


def _cumsum_rows(x):
    row = lax.broadcasted_iota(jnp.int32, x.shape, 0)
    s = 1
    while s < x.shape[0]:
        x = x + jnp.where(row >= s, pltpu.roll(x, s, axis=0), 0.0)
        s *= 2
    return x


def _softplus(x):
    return jnp.maximum(x, 0.0) + jnp.log1p(jnp.exp(-jnp.abs(x)))


def _gdn_kernel(qkv_ref, tail_ref, last_ref, convw_ref, alog_ref, dtb_ref,
                yr_ref, gng_ref, o_ref, state_ref, xext_ref):
    C = CHUNK
    HALO = 8
    NB = GDN_STEP_CHUNKS
    NP = GDN_HEADS // 2
    DK, DV = GDN_DK, GDN_DV

    @pl.when(pl.program_id(1) == 0)
    def _():
        state_ref[...] = jnp.zeros_like(state_ref)
        xext_ref[0:HALO, :] = jnp.zeros((HALO, GDN_QKV), F32)

    xext_ref[HALO:HALO + NB * C, :] = qkv_ref[...]

    gng = gng_ref[...]
    row = lax.broadcasted_iota(jnp.int32, (C, 2 * C), 0)
    lane = lax.broadcasted_iota(jnp.int32, (C, 2 * C), 1)
    col = lane & (C - 1)
    lo = lane < C
    lo_t = lax.broadcasted_iota(jnp.int32, (2 * C, 2 * C), 1) < C
    causal = row >= col
    strict = row > col
    lvl_masks = []
    for b in range(C.bit_length() - 1):
        ib = lax.shift_right_logical(row, b)
        jb = lax.shift_right_logical(col, b)
        lvl_masks.append(((ib & 1) == 1) & (jb == ib - 1))

    def conv_act(ci, col0):
        sl = slice(col0, col0 + LANE)
        r0 = HALO + ci * C
        y = convw_ref[CONV_K - 1:CONV_K, sl] * xext_ref[r0:r0 + C, sl]
        for d in range(1, CONV_K):
            y = y + convw_ref[CONV_K - 1 - d:CONV_K - d, sl] * xext_ref[r0 - d:r0 - d + C, sl]
        return _silu(y)

    def block_diag(x, mask):
        return jnp.concatenate([jnp.where(mask, x, 0.0), jnp.where(mask, 0.0, x)], axis=0).astype(BF16)

    def block_diag_wide(xa, xb):
        za = jnp.zeros_like(xa)
        top = jnp.concatenate([xa, za], axis=1)
        bot = jnp.concatenate([za, xb], axis=1)
        return jnp.concatenate([top, bot], axis=0).astype(BF16)

    def tail_cols(ci, off):
        rs = slice(ci * C, (ci + 1) * C)
        lo_c = off - OFF_GA
        n_main = min(LANE, MAIN_WIDTH - off)
        if n_main == LANE:
            return tail_ref[rs, lo_c:lo_c + LANE]
        return jnp.concatenate([tail_ref[rs, lo_c:lo_c + n_main], last_ref[rs, 0:LANE - n_main]], axis=1)

    def bcast_col(x, j):
        return jnp.broadcast_to(x[:, j:j + 1], (C, LANE))

    chunks = range(NB)
    heads = range(GDN_HEADS)
    items = [(ci, p) for ci in chunks for p in range(NP)]

    cg, cg_t, bt = [], [], []
    for ci in chunks:
        sm = tail_ref[ci * C:(ci + 1) * C, 0:LANE]
        gl = -jnp.exp(alog_ref[...]) * _softplus(sm + dtb_ref[...])
        cg.append(_cumsum_rows(gl))
        cg_t.append(cg[ci].T)
        bt.append(_sigmoid(sm))

    qn, kn, vv, cgc, btc, ecg, kb = {}, {}, {}, {}, {}, {}, {}
    for ci in chunks:
        for h in heads:
            q = conv_act(ci, h * DK)
            k = conv_act(ci, GDN_HEADS * DK + h * DK)
            vv[ci, h] = conv_act(ci, 2 * GDN_HEADS * DK + h * DV)
            qn[ci, h] = q * lax.rsqrt(jnp.sum(q * q, axis=-1, keepdims=True) + 1e-6) * (DK ** -0.5)
            kn[ci, h] = k * lax.rsqrt(jnp.sum(k * k, axis=-1, keepdims=True) + 1e-6)
    for ci in chunks:
        for h in heads:
            cgc[ci, h] = bcast_col(cg[ci], h)
            btc[ci, h] = bcast_col(bt[ci], GDN_HEADS + h)
            ecg[ci, h] = jnp.exp(cgc[ci, h])
            kb[ci, h] = kn[ci, h] * btc[ci, h]

    decay, knt, m, qk = {}, {}, {}, {}
    for it in items:
        ci, p = it
        a, b = (ci, 2 * p), (ci, 2 * p + 1)
        cgc_p = jnp.where(lo, cgc[a], cgc[b])
        cgr_p = jnp.concatenate([cg_t[ci][2 * p:2 * p + 1, :], cg_t[ci][2 * p + 1:2 * p + 2, :]], axis=1)
        decay[it] = jnp.where(causal, jnp.exp(jnp.where(causal, cgc_p - cgr_p, 0.0)), 0.0)
        knt[it] = jnp.concatenate([kn[a], kn[b]], axis=0).T
    for it in items:
        ci, p = it
        a, b = (ci, 2 * p), (ci, 2 * p + 1)
        lhs = jnp.concatenate([jnp.concatenate([kb[a], kb[b]], axis=1),
                               jnp.concatenate([qn[a], qn[b]], axis=1)], axis=0)
        kq = _dot(lhs.astype(BF16), block_diag(knt[it], lo_t))
        m[it] = jnp.where(strict, kq[:C] * decay[it], 0.0)
        qk[it] = kq[C:] * decay[it]

    r = {it: jnp.where(lvl_masks[0], -m[it], 0.0) for it in items}
    for lm in lvl_masks[1:]:
        cm = {it: jnp.where(lm, m[it], 0.0) for it in items}
        y = {it: cm[it] + _dot(r[it].astype(BF16), block_diag(cm[it], lo)) for it in items}
        r = {it: r[it] - (y[it] + _dot(y[it].astype(BF16), block_diag(r[it], lo))) for it in items}

    u, w = {}, {}
    for it in items:
        ci, p = it
        a, b = (ci, 2 * p), (ci, 2 * p + 1)
        xa = jnp.concatenate([vv[a] * btc[a], kb[a] * ecg[a]], axis=1)
        xb = jnp.concatenate([vv[b] * btc[b], kb[b] * ecg[b]], axis=1)
        rx = _dot(r[it].astype(BF16), block_diag_wide(xa, xb))
        u[it] = (jnp.concatenate([xa[:, :DV], xb[:, :DV]], axis=1) +
                 jnp.concatenate([rx[:, :DV], rx[:, DV + DK:2 * DV + DK]], axis=1))
        w[it] = (jnp.concatenate([xa[:, DV:], xb[:, DV:]], axis=1) +
                 jnp.concatenate([rx[:, DV:DV + DK], rx[:, 2 * DV + DK:]], axis=1))

    state = [state_ref[p] for p in range(NP)]
    for ci in chunks:
        wsqs, v_new = {}, {}
        for p in range(NP):
            it = (ci, p)
            a, b = (ci, 2 * p), (ci, 2 * p + 1)
            qe = jnp.concatenate([qn[a] * ecg[a], qn[b] * ecg[b]], axis=1)
            lhs = jnp.concatenate([w[it], qe], axis=0)
            wsqs[p] = _dot(lhs.astype(BF16), block_diag_wide(state[p][:, :DV], state[p][:, DV:]))
            v_new[p] = u[it] - wsqs[p][:C]
        for p in range(NP):
            it = (ci, p)
            a, b = (ci, 2 * p), (ci, 2 * p + 1)
            g_last_a = cgc[a][C - 1:C, :]
            g_last_b = cgc[b][C - 1:C, :]
            edr = jnp.exp(jnp.concatenate([g_last_a[:, :C] - cg_t[ci][2 * p:2 * p + 1, :],
                                           g_last_b[:, :C] - cg_t[ci][2 * p + 1:2 * p + 2, :]], axis=1))
            lhs = jnp.concatenate([qk[it], knt[it] * edr], axis=0)
            res = _dot(lhs.astype(BF16), block_diag_wide(v_new[p][:, :DV], v_new[p][:, DV:]))
            eg = jnp.exp(jnp.concatenate([g_last_a, g_last_b], axis=1))
            state[p] = state[p] * eg + res[C:]
            o_pair = wsqs[p][C:] + res[:C]
            for i, h in enumerate((2 * p, 2 * p + 1)):
                hs = slice(h * DV, (h + 1) * DV)
                rs = slice(ci * C, (ci + 1) * C)
                o = o_pair[:, i * DV:(i + 1) * DV]
                on = o * lax.rsqrt(jnp.mean(o * o, axis=-1, keepdims=True) + NORM_EPS) * gng
                yg = _silu(tail_cols(ci, OFF_GZ + h * DV)) * on
                merged = (_sigmoid(tail_cols(ci, OFF_GATE_A + h * DV)) * yr_ref[rs, hs]
                          + _sigmoid(tail_cols(ci, OFF_GATE_B + h * DV)) * yg)
                o_ref[rs, hs] = merged.astype(o_ref.dtype)
    for p in range(NP):
        state_ref[p] = state[p]

    xext_ref[0:HALO, :] = xext_ref[NB * C:NB * C + HALO, :]


def _gdn_merge(proj, last, yr, conv_w, a_log, dt_bias, gdn_norm_g, batch, seq):
    t = batch * seq
    R = GDN_STEP_CHUNKS * CHUNK
    ns = seq // R
    pad = LANE - GDN_HEADS
    alog_p = jnp.pad(a_log.astype(F32), (0, pad)).reshape(1, LANE)
    dtb_p = jnp.pad(dt_bias.astype(F32), (0, pad)).reshape(1, LANE)
    tail_w = MAIN_WIDTH - OFF_GA
    assert OFF_GQKV % GDN_QKV == 0 and OFF_GA % tail_w == 0 and seq % R == 0

    def row(b, c):
        return b * ns + c

    return pl.pallas_call(
        _gdn_kernel,
        grid=(batch, ns),
        in_specs=[
            pl.BlockSpec((R, GDN_QKV), lambda b, c: (row(b, c), OFF_GQKV // GDN_QKV)),
            pl.BlockSpec((R, tail_w), lambda b, c: (row(b, c), OFF_GA // tail_w)),
            pl.BlockSpec((R, LANE), lambda b, c: (row(b, c), 0)),
            pl.BlockSpec((CONV_K, GDN_QKV), lambda b, c: (0, 0)),
            pl.BlockSpec((1, LANE), lambda b, c: (0, 0)),
            pl.BlockSpec((1, LANE), lambda b, c: (0, 0)),
            pl.BlockSpec((R, D_MODEL), lambda b, c: (row(b, c), 0)),
            pl.BlockSpec((1, GDN_DV), lambda b, c: (0, 0)),
        ],
        out_specs=pl.BlockSpec((R, D_MODEL), lambda b, c: (row(b, c), 0)),
        out_shape=jax.ShapeDtypeStruct((t, D_MODEL), BF16),
        scratch_shapes=[pltpu.VMEM((GDN_HEADS // 2, GDN_DK, 2 * GDN_DV), F32),
                        pltpu.VMEM((R + 8, GDN_QKV), F32)],
        compiler_params=_cparams(("arbitrary", "arbitrary")),
        name="gdn_merge",
    )(proj, proj, last, conv_w, alog_p, dtb_p, yr, gdn_norm_g.reshape(1, GDN_DV))


def _xa_fold_q_kernel(wq_ref, k_ref, o_ref):
    a = _dot_nt(wq_ref[...].astype(BF16), k_ref[...]) * (XATTN_DH ** -0.5)
    o_ref[...] = a.astype(o_ref.dtype)


def _xa_fold_o_kernel(v_ref, wo_ref, o_ref):
    o_ref[...] = _dot(v_ref[...], wo_ref[...].astype(BF16)).astype(o_ref.dtype)


def _xa_main_kernel(h_ref, a_ref, b_ref, r_ref, g_ref, o_ref, hn_ref, *, mem_tokens):
    s = _dot(h_ref[...], a_ref[...])
    ps = []
    for hh in range(XATTN_HEADS):
        seg = s[:, hh * mem_tokens:(hh + 1) * mem_tokens]
        p = jnp.exp(seg - jnp.max(seg, axis=-1, keepdims=True))
        ps.append((p / jnp.sum(p, axis=-1, keepdims=True)).astype(BF16))
    x = _dot(jnp.concatenate(ps, axis=1), b_ref[...]) + r_ref[...]
    o_ref[...] = x
    ms = jnp.mean(x * x, axis=-1, keepdims=True)
    hn_ref[...] = (x * lax.rsqrt(ms + NORM_EPS) * g_ref[...]).astype(hn_ref.dtype)


def _xattn_residual(h, kv, w_xq, w_xo, layer, x2, next_g, batch, seq, mem_tokens, tq=512):
    t, d = h.shape
    hm = XATTN_HEADS * mem_tokens
    nq = seq // tq
    grid_bh = (batch, XATTN_HEADS)
    fold_q = pl.pallas_call(
        _xa_fold_q_kernel,
        grid=grid_bh,
        in_specs=[pl.BlockSpec((None, d, XATTN_DH), lambda b, hh: (layer, 0, hh)),
                  pl.BlockSpec((mem_tokens, XATTN_DH), lambda b, hh: (b, hh))],
        out_specs=pl.BlockSpec((None, d, mem_tokens), lambda b, hh: (b, 0, hh)),
        out_shape=jax.ShapeDtypeStruct((batch, d, hm), BF16),
        compiler_params=_cparams(("parallel", "parallel")),
        name="xattn_fold_q",
    )(w_xq, kv)
    fold_o = pl.pallas_call(
        _xa_fold_o_kernel,
        grid=grid_bh,
        in_specs=[pl.BlockSpec((mem_tokens, XATTN_DH), lambda b, hh: (b, XATTN_HEADS + hh)),
                  pl.BlockSpec((None, XATTN_DH, d), lambda b, hh: (layer, hh, 0))],
        out_specs=pl.BlockSpec((None, mem_tokens, d), lambda b, hh: (b, hh, 0)),
        out_shape=jax.ShapeDtypeStruct((batch, hm, d), BF16),
        compiler_params=_cparams(("parallel", "parallel")),
        name="xattn_fold_o",
    )(kv, w_xo)
    return pl.pallas_call(
        functools.partial(_xa_main_kernel, mem_tokens=mem_tokens),
        grid=(batch, nq),
        in_specs=[pl.BlockSpec((tq, d), lambda b, i: (b * nq + i, 0)),
                  pl.BlockSpec((None, d, hm), lambda b, i: (b, 0, 0)),
                  pl.BlockSpec((None, hm, d), lambda b, i: (b, 0, 0)),
                  pl.BlockSpec((tq, d), lambda b, i: (b * nq + i, 0)),
                  pl.BlockSpec((1, d), lambda b, i: (0, 0))],
        out_specs=[pl.BlockSpec((tq, d), lambda b, i: (b * nq + i, 0))] * 2,
        out_shape=[jax.ShapeDtypeStruct((t, d), F32), jax.ShapeDtypeStruct((t, d), BF16)],
        compiler_params=_cparams(("parallel", "parallel")),
        name="xattn",
    )(h, fold_q, fold_o, x2, next_g.reshape(1, d))


def kernel(x, mem, positions, norm_mix_g, w_in, conv_w, gdn_a_log, gdn_dt_bias, ret_gn_g, gdn_norm_g, w_out, norm_x_g, norm_mem_g, w_xq, w_xkv, w_xo, norm_ffn_g, w_ff1, w_ff2, norm_final_g):
    batch, seq, d = x.shape
    mem_tokens = mem.shape[1]
    depth = w_in.shape[0]
    t = batch * seq
    x2 = x.reshape(t, d)
    mem2 = mem.reshape(batch * mem_tokens, d)

    cos2, sin2 = _rope_tables(positions)
    ret_tables = _retention_tables()
    w_ff2_b = w_ff2.astype(BF16)
    w_in_nk = jnp.swapaxes(w_in, 1, 2)

    for l in range(depth):
        h = _rmsnorm(x2, norm_mix_g[l], BF16)
        proj = _matmul(h, w_in_nk, l, MAIN_WIDTH, F32, tm=1024, tn=PROJ_TN, w_is_nk=True)
        last = _matmul(h, w_in_nk, l, LANE, F32, tm=1024, tn=LANE, col0=MAIN_WIDTH,
                       n_valid=IN_WIDTH - MAIN_WIDTH, w_is_nk=True)
        yr = _retention(proj, cos2, sin2, ret_tables, ret_gn_g[l], batch, seq)
        merged = _gdn_merge(proj, last, yr, conv_w[l], gdn_a_log[l], gdn_dt_bias[l],
                            gdn_norm_g[l], batch, seq)
        x2 = _matmul(merged, w_out, l, d, F32, tm=1024, tn=1024, residual=x2)

        h = _rmsnorm(x2, norm_x_g[l], BF16)
        mem_n = _rmsnorm(mem2, norm_mem_g[l], BF16)
        kv = _matmul(mem_n, w_xkv, l, 2 * d, BF16, tm=batch * mem_tokens, tn=1024)
        x2, h = _xattn_residual(h, kv, w_xq, w_xo, l, x2, norm_ffn_g[l], batch, seq, mem_tokens)
        hid = _matmul(h, w_ff1, l, D_FF, BF16, tm=1024, tn=1024, relu2=True)
        x2 = _matmul(hid, w_ff2_b, l, d, F32, tm=512, tn=512, residual=x2)

    out = _rmsnorm(x2, norm_final_g, x.dtype)
    return out.reshape(batch, seq, d)
```

```python
import functools

import numpy as np
import jax
import jax.numpy as jnp
from jax import lax
from jax.experimental import pallas as pl
from jax.experimental.pallas import tpu as pltpu

F32 = jnp.float32
BF16 = jnp.bfloat16

D_MODEL = 2048
CHUNK = 64
RET_HEADS = 8
RET_DK = D_MODEL // 16
RET_DV = D_MODEL // RET_HEADS
GDN_HEADS = 16
GDN_DK = D_MODEL // GDN_HEADS
GDN_DV = D_MODEL // GDN_HEADS
CONV_K = 4
XATTN_HEADS = 4
XATTN_DH = D_MODEL // XATTN_HEADS
D_FF = 4 * D_MODEL
ROPE_THETA = 10000.0
NORM_EPS = 1e-6
GDN_QKV = 3 * GDN_HEADS * GDN_DK
OFF_RQ = 0
OFF_RK = OFF_RQ + RET_HEADS * RET_DK
OFF_RV = OFF_RK + RET_HEADS * RET_DK
OFF_RG = OFF_RV + RET_HEADS * RET_DV
OFF_GQKV = OFF_RG + RET_HEADS * RET_DV
OFF_GA = OFF_GQKV + GDN_QKV
OFF_GB = OFF_GA + GDN_HEADS
OFF_GZ = OFF_GB + GDN_HEADS
OFF_GATE_A = OFF_GZ + GDN_HEADS * GDN_DV
OFF_GATE_B = OFF_GATE_A + D_MODEL
IN_WIDTH = OFF_GATE_B + D_MODEL

LANE = 128
PROJ_TN = 1024
TAIL_WIDTH = IN_WIDTH - OFF_GZ
RET_BLOCK = 256
GDN_STEP_CHUNKS = 2
VMEM_LIMIT = 56 * 1024 * 1024


def _cparams(sem):
    return pltpu.CompilerParams(dimension_semantics=sem, vmem_limit_bytes=VMEM_LIMIT)


def _dot(a, b):
    return jnp.dot(a, b, preferred_element_type=F32)


def _dot_nt(a, b):
    return lax.dot_general(a, b, (((1,), (1,)), ((), ())), preferred_element_type=F32)


def _dot_tn(a, b):
    return lax.dot_general(a, b, (((0,), (0,)), ((), ())), preferred_element_type=F32)


NEG_LOG2E = -1.4426950408889634


def _sigmoid(x):
    return 1.0 / (1.0 + jnp.exp2(x * NEG_LOG2E))


def _silu(x):
    return x * _sigmoid(x)


def _rmsnorm_kernel(x_ref, g_ref, o_ref):
    x = x_ref[...]
    ms = jnp.mean(x * x, axis=-1, keepdims=True)
    o_ref[...] = (x * lax.rsqrt(ms + NORM_EPS) * g_ref[...]).astype(o_ref.dtype)


def _rmsnorm(x, g, out_dtype, tm=512):
    m, d = x.shape
    tm = min(tm, m)
    return pl.pallas_call(
        _rmsnorm_kernel,
        grid=(m // tm,),
        in_specs=[pl.BlockSpec((tm, d), lambda i: (i, 0)),
                  pl.BlockSpec((1, d), lambda i: (0, 0))],
        out_specs=pl.BlockSpec((tm, d), lambda i: (i, 0)),
        out_shape=jax.ShapeDtypeStruct((m, d), out_dtype),
        compiler_params=_cparams(("parallel",)),
        name="rmsnorm",
    )(x, g.reshape(1, d))


def _mm_kernel(*refs, relu2, has_res, cast_w, w_is_nk):
    it = iter(refs)
    a_ref = next(it)
    w_ref = next(it)
    r_ref = next(it) if has_res else None
    o_ref = next(it)
    wbf_ref = next(it) if cast_w else w_ref

    if cast_w:
        @pl.when(pl.program_id(1) == 0)
        def _():
            if w_is_nk:
                w = w_ref[0].T
            else:
                w = w_ref[...]
            wbf_ref[...] = w.astype(BF16)

    acc = _dot(a_ref[...], wbf_ref[...])
    if relu2:
        acc = jnp.square(jnp.maximum(acc, 0.0))
    if has_res:
        acc = acc + r_ref[...]
    o_ref[...] = acc.astype(o_ref.dtype)


def _matmul(a, w, layer, n_out, out_dtype, tm, tn, relu2=False, residual=None, col0=0,
            w_is_nk=False):
    m, k = a.shape
    assert w.shape[2 if w_is_nk else 1] == k and n_out % tn == 0 and m % tm == 0
    assert col0 + n_out <= w.shape[1 if w_is_nk else 2]
    cast_w = w.dtype != BF16 or w_is_nk
    if w_is_nk:
        assert col0 % 8 == 0
        w_spec = pl.BlockSpec((pl.Element(1), pl.Element(tn), pl.Element(k)),
                              lambda j, i: (layer, pl.multiple_of(col0 + j * tn, 8), 0))
    else:
        assert col0 % tn == 0
        w_spec = pl.BlockSpec((None, k, tn), lambda j, i: (layer, 0, col0 // tn + j))
    in_specs = [pl.BlockSpec((tm, k), lambda j, i: (i, 0)), w_spec]
    args = [a, w]
    if residual is not None:
        in_specs.append(pl.BlockSpec((tm, tn), lambda j, i: (i, j)))
        args.append(residual)
    return pl.pallas_call(
        functools.partial(_mm_kernel, relu2=relu2, has_res=residual is not None, cast_w=cast_w,
                          w_is_nk=w_is_nk),
        grid=(n_out // tn, m // tm),
        in_specs=in_specs,
        out_specs=pl.BlockSpec((tm, tn), lambda j, i: (i, j)),
        out_shape=jax.ShapeDtypeStruct((m, n_out), out_dtype),
        scratch_shapes=[pltpu.VMEM((k, tn), BF16)] if cast_w else [],
        compiler_params=_cparams(("arbitrary", "arbitrary")),
        name="matmul",
    )(*args)


def _rope_kernel(pos_ref, invf_ref, sgn_ref, cos_ref, sin_ref):
    ang = pos_ref[...] * invf_ref[...]
    cos_ref[...] = jnp.cos(ang)
    sin_ref[...] = jnp.sin(ang) * sgn_ref[...]


def _rope_tables(positions):
    t = positions.size
    half = RET_DK // 2
    inv_freq = 1.0 / (ROPE_THETA ** (jnp.arange(0, RET_DK, 2, dtype=F32) / RET_DK))
    invf2 = jnp.concatenate([inv_freq, inv_freq]).reshape(1, RET_DK)
    sgn = jnp.concatenate([-jnp.ones((half,), F32), jnp.ones((half,), F32)]).reshape(1, RET_DK)
    pos = positions.astype(F32).reshape(t, 1)
    tm = 1024
    return pl.pallas_call(
        _rope_kernel,
        grid=(t // tm,),
        in_specs=[pl.BlockSpec((tm, 1), lambda i: (i, 0)),
                  pl.BlockSpec((1, RET_DK), lambda i: (0, 0)),
                  pl.BlockSpec((1, RET_DK), lambda i: (0, 0))],
        out_specs=[pl.BlockSpec((tm, RET_DK), lambda i: (i, 0))] * 2,
        out_shape=[jax.ShapeDtypeStruct((t, RET_DK), F32)] * 2,
        compiler_params=_cparams(("parallel",)),
        name="rope_tables",
    )(pos, invf2, sgn)


def _retention_tables():
    idx = np.arange(RET_BLOCK, dtype=np.float64)
    log_gamma = jnp.log1p(-jnp.exp2(-5.0 - jnp.arange(RET_HEADS, dtype=F32)))
    lg = log_gamma[:, None, None]
    dist = jnp.asarray(np.abs(idx[:, None] - idx[None, :]), F32)
    visible = jnp.asarray((idx[None, :] // CHUNK) <= (idx[:, None] // CHUNK))
    wmat = jnp.where(visible[None], jnp.exp(lg * dist[None]), 0.0)
    qd = jnp.exp(log_gamma[:, None] * jnp.asarray(idx + 1.0, F32))
    kd = jnp.exp(log_gamma[:, None] * jnp.asarray(RET_BLOCK - 1.0 - idx, F32))
    cd = jnp.exp(log_gamma * RET_BLOCK)
    qd = jnp.broadcast_to(qd[:, :, None], (RET_HEADS, RET_BLOCK, RET_DK))
    kd = jnp.broadcast_to(kd[:, :, None], (RET_HEADS, RET_BLOCK, RET_DK))
    cd = jnp.broadcast_to(cd[:, None, None], (RET_HEADS, 8, RET_DV))
    return wmat, qd, kd, cd


def _retention_kernel(q_ref, k_ref, v_ref, rg_ref, cos_ref, sin_ref, w_ref, qd_ref, kd_ref,
                      cd_ref, gn_ref, o_ref, state_ref):
    @pl.when(pl.program_id(1) == 0)
    def _():
        state_ref[...] = jnp.zeros_like(state_ref)

    cos = cos_ref[...]
    sin = sin_ref[...]
    half = RET_DK // 2
    heads = range(RET_HEADS)
    qr, kr, vb = [], [], []
    for h in heads:
        ks = slice(h * RET_DK, (h + 1) * RET_DK)
        q = q_ref[:, ks]
        k = k_ref[:, ks]
        qr.append(q * cos + pltpu.roll(q, half, axis=1) * sin)
        kr.append((k * cos + pltpu.roll(k, half, axis=1) * sin) * (RET_DK ** -0.5))
        vb.append(v_ref[:, h * RET_DV:(h + 1) * RET_DV].astype(BF16))
    scores = [_dot_nt(qr[h].astype(BF16), kr[h].astype(BF16)) * w_ref[h] for h in heads]
    s_prev = [state_ref[h] for h in heads]
    o = [_dot(scores[h].astype(BF16), vb[h])
         + _dot((qr[h] * qd_ref[h]).astype(BF16), s_prev[h].astype(BF16)) for h in heads]
    for h in heads:
        state_ref[h] = (s_prev[h] * cd_ref[h][0:1, :]
                        + _dot_tn((kr[h] * kd_ref[h]).astype(BF16), vb[h]))
    for h in heads:
        vs = slice(h * RET_DV, (h + 1) * RET_DV)
        mu = jnp.mean(o[h], axis=-1, keepdims=True)
        oc = o[h] - mu
        var = jnp.mean(oc * oc, axis=-1, keepdims=True)
        y = oc * lax.rsqrt(var + 1e-5) * gn_ref[:, vs]
        o_ref[:, vs] = _silu(rg_ref[:, vs]) * y


def _retention(proj, cos2, sin2, tables, gn_g, batch, seq):
    wmat, qd, kd, cd = tables
    t = batch * seq
    nb = seq // RET_BLOCK
    L = RET_BLOCK
    qk_w = RET_HEADS * RET_DK
    v_w = RET_HEADS * RET_DV
    assert OFF_RQ % qk_w == 0 and OFF_RK % qk_w == 0 and OFF_RV % v_w == 0 and OFF_RG % v_w == 0

    def row(b, s):
        return b * nb + s

    def whole(x):
        return pl.BlockSpec(x.shape, lambda b, s: (0,) * x.ndim)

    gn2 = gn_g.reshape(1, v_w)
    return pl.pallas_call(
        _retention_kernel,
        grid=(batch, nb),
        in_specs=[
            pl.BlockSpec((L, qk_w), lambda b, s: (row(b, s), OFF_RQ // qk_w)),
            pl.BlockSpec((L, qk_w), lambda b, s: (row(b, s), OFF_RK // qk_w)),
            pl.BlockSpec((L, v_w), lambda b, s: (row(b, s), OFF_RV // v_w)),
            pl.BlockSpec((L, v_w), lambda b, s: (row(b, s), OFF_RG // v_w)),
            pl.BlockSpec((L, RET_DK), lambda b, s: (row(b, s), 0)),
            pl.BlockSpec((L, RET_DK), lambda b, s: (row(b, s), 0)),
            whole(wmat), whole(qd), whole(kd), whole(cd), whole(gn2),
        ],
        out_specs=pl.BlockSpec((L, v_w), lambda b, s: (row(b, s), 0)),
        out_shape=jax.ShapeDtypeStruct((t, v_w), F32),
        scratch_shapes=[pltpu.VMEM((RET_HEADS, RET_DK, RET_DV), F32)],
        compiler_params=_cparams(("arbitrary", "arbitrary")),
        name="retention",
    )(proj, proj, proj, proj, cos2, sin2, wmat, qd, kd, cd, gn2)


def _cumsum_rows(x):
    row = lax.broadcasted_iota(jnp.int32, x.shape, 0)
    s = 1
    while s < x.shape[0]:
        x = x + jnp.where(row >= s, pltpu.roll(x, s, axis=0), 0.0)
        s *= 2
    return x


def _softplus(x):
    return jnp.maximum(x, 0.0) + jnp.log1p(jnp.exp(-jnp.abs(x)))


def _gdn_kernel(qkv_ref, gates_ref, convw_ref, alog_ref, dtb_ref, o_ref, state_ref, xext_ref):
    C = CHUNK
    HALO = 8
    NB = GDN_STEP_CHUNKS
    NP = GDN_HEADS // 2
    DK, DV = GDN_DK, GDN_DV

    @pl.when(pl.program_id(1) == 0)
    def _():
        state_ref[...] = jnp.zeros_like(state_ref)
        xext_ref[0:HALO, :] = jnp.zeros((HALO, GDN_QKV), F32)

    xext_ref[HALO:HALO + NB * C, :] = qkv_ref[...]

    row = lax.broadcasted_iota(jnp.int32, (C, 2 * C), 0)
    lane = lax.broadcasted_iota(jnp.int32, (C, 2 * C), 1)
    col = lane & (C - 1)
    lo = lane < C
    lo_t = lax.broadcasted_iota(jnp.int32, (2 * C, 2 * C), 1) < C
    causal = row >= col
    strict = row > col
    lvl_masks = []
    for b in range(C.bit_length() - 1):
        ib = lax.shift_right_logical(row, b)
        jb = lax.shift_right_logical(col, b)
        lvl_masks.append(((ib & 1) == 1) & (jb == ib - 1))

    def conv_act(ci, col0):
        sl = slice(col0, col0 + LANE)
        r0 = HALO + ci * C
        y = convw_ref[CONV_K - 1:CONV_K, sl] * xext_ref[r0:r0 + C, sl]
        for d in range(1, CONV_K):
            y = y + convw_ref[CONV_K - 1 - d:CONV_K - d, sl] * xext_ref[r0 - d:r0 - d + C, sl]
        return _silu(y)

    def block_diag(x, mask):
        return jnp.concatenate([jnp.where(mask, x, 0.0), jnp.where(mask, 0.0, x)], axis=0).astype(BF16)

    def block_diag_wide(xa, xb):
        za = jnp.zeros_like(xa)
        top = jnp.concatenate([xa, za], axis=1)
        bot = jnp.concatenate([za, xb], axis=1)
        return jnp.concatenate([top, bot], axis=0).astype(BF16)

    def bcast_col(x, j):
        return jnp.broadcast_to(x[:, j:j + 1], (C, LANE))

    chunks = range(NB)
    heads = range(GDN_HEADS)
    items = [(ci, p) for ci in chunks for p in range(NP)]

    cg, cg_t, bt = [], [], []
    for ci in chunks:
        sm = gates_ref[ci * C:(ci + 1) * C, :]
        gl = -jnp.exp(alog_ref[...]) * _softplus(sm + dtb_ref[...])
        cg.append(_cumsum_rows(gl))
        cg_t.append(cg[ci].T)
        bt.append(_sigmoid(sm))

    qn, kn, vv, cgc, btc, ecg, kb = {}, {}, {}, {}, {}, {}, {}
    for ci in chunks:
        for h in heads:
            q = conv_act(ci, h * DK)
            k = conv_act(ci, GDN_HEADS * DK + h * DK)
            vv[ci, h] = conv_act(ci, 2 * GDN_HEADS * DK + h * DV)
            qn[ci, h] = q * lax.rsqrt(jnp.sum(q * q, axis=-1, keepdims=True) + 1e-6) * (DK ** -0.5)
            kn[ci, h] = k * lax.rsqrt(jnp.sum(k * k, axis=-1, keepdims=True) + 1e-6)
    for ci in chunks:
        for h in heads:
            cgc[ci, h] = bcast_col(cg[ci], h)
            btc[ci, h] = bcast_col(bt[ci], GDN_HEADS + h)
            ecg[ci, h] = jnp.exp(cgc[ci, h])
            kb[ci, h] = kn[ci, h] * btc[ci, h]

    decay, knt, m, qk = {}, {}, {}, {}
    for it in items:
        ci, p = it
        a, b = (ci, 2 * p), (ci, 2 * p + 1)
        cgc_p = jnp.where(lo, cgc[a], cgc[b])
        cgr_p = jnp.concatenate([cg_t[ci][2 * p:2 * p + 1, :], cg_t[ci][2 * p + 1:2 * p + 2, :]], axis=1)
        decay[it] = jnp.where(causal, jnp.exp(jnp.where(causal, cgc_p - cgr_p, 0.0)), 0.0)
        knt[it] = jnp.concatenate([kn[a], kn[b]], axis=0).T
    for it in items:
        ci, p = it
        a, b = (ci, 2 * p), (ci, 2 * p + 1)
        lhs = jnp.concatenate([jnp.concatenate([kb[a], kb[b]], axis=1),
                               jnp.concatenate([qn[a], qn[b]], axis=1)], axis=0)
        kq = _dot(lhs.astype(BF16), block_diag(knt[it], lo_t))
        m[it] = jnp.where(strict, kq[:C] * decay[it], 0.0)
        qk[it] = kq[C:] * decay[it]

    r = {it: jnp.where(lvl_masks[0], -m[it], 0.0) for it in items}
    for lm in lvl_masks[1:]:
        cm = {it: jnp.where(lm, m[it], 0.0) for it in items}
        y = {it: cm[it] + _dot(r[it].astype(BF16), block_diag(cm[it], lo)) for it in items}
        r = {it: r[it] - (y[it] + _dot(y[it].astype(BF16), block_diag(r[it], lo))) for it in items}

    u, w = {}, {}
    for it in items:
        ci, p = it
        a, b = (ci, 2 * p), (ci, 2 * p + 1)
        xa = jnp.concatenate([vv[a] * btc[a], kb[a] * ecg[a]], axis=1)
        xb = jnp.concatenate([vv[b] * btc[b], kb[b] * ecg[b]], axis=1)
        rx = _dot(r[it].astype(BF16), block_diag_wide(xa, xb))
        u[it] = (jnp.concatenate([xa[:, :DV], xb[:, :DV]], axis=1) +
                 jnp.concatenate([rx[:, :DV], rx[:, DV + DK:2 * DV + DK]], axis=1))
        w[it] = (jnp.concatenate([xa[:, DV:], xb[:, DV:]], axis=1) +
                 jnp.concatenate([rx[:, DV:DV + DK], rx[:, 2 * DV + DK:]], axis=1))

    state = [state_ref[p] for p in range(NP)]
    for ci in chunks:
        wsqs, v_new = {}, {}
        for p in range(NP):
            it = (ci, p)
            a, b = (ci, 2 * p), (ci, 2 * p + 1)
            qe = jnp.concatenate([qn[a] * ecg[a], qn[b] * ecg[b]], axis=1)
            lhs = jnp.concatenate([w[it], qe], axis=0)
            wsqs[p] = _dot(lhs.astype(BF16), block_diag_wide(state[p][:, :DV], state[p][:, DV:]))
            v_new[p] = u[it] - wsqs[p][:C]
        for p in range(NP):
            it = (ci, p)
            a, b = (ci, 2 * p), (ci, 2 * p + 1)
            g_last_a = cgc[a][C - 1:C, :]
            g_last_b = cgc[b][C - 1:C, :]
            edr = jnp.exp(jnp.concatenate([g_last_a[:, :C] - cg_t[ci][2 * p:2 * p + 1, :],
                                           g_last_b[:, :C] - cg_t[ci][2 * p + 1:2 * p + 2, :]], axis=1))
            lhs = jnp.concatenate([qk[it], knt[it] * edr], axis=0)
            res = _dot(lhs.astype(BF16), block_diag_wide(v_new[p][:, :DV], v_new[p][:, DV:]))
            eg = jnp.exp(jnp.concatenate([g_last_a, g_last_b], axis=1))
            state[p] = state[p] * eg + res[C:]
            o_ref[ci * C:(ci + 1) * C, 2 * p * DV:(2 * p + 2) * DV] = wsqs[p][C:] + res[:C]
    for p in range(NP):
        state_ref[p] = state[p]

    xext_ref[0:HALO, :] = xext_ref[NB * C:NB * C + HALO, :]


def _gdn(proj, gates, conv_w, a_log, dt_bias, batch, seq):
    t = batch * seq
    R = GDN_STEP_CHUNKS * CHUNK
    ns = seq // R
    pad = LANE - GDN_HEADS
    alog_p = jnp.pad(a_log.astype(F32), (0, pad)).reshape(1, LANE)
    dtb_p = jnp.pad(dt_bias.astype(F32), (0, pad)).reshape(1, LANE)
    assert OFF_GQKV % GDN_QKV == 0 and seq % R == 0

    def row(b, c):
        return b * ns + c

    return pl.pallas_call(
        _gdn_kernel,
        grid=(batch, ns),
        in_specs=[
            pl.BlockSpec((R, GDN_QKV), lambda b, c: (row(b, c), OFF_GQKV // GDN_QKV)),
            pl.BlockSpec((R, LANE), lambda b, c: (row(b, c), 0)),
            pl.BlockSpec((CONV_K, GDN_QKV), lambda b, c: (0, 0)),
            pl.BlockSpec((1, LANE), lambda b, c: (0, 0)),
            pl.BlockSpec((1, LANE), lambda b, c: (0, 0)),
        ],
        out_specs=pl.BlockSpec((R, D_MODEL), lambda b, c: (row(b, c), 0)),
        out_shape=jax.ShapeDtypeStruct((t, D_MODEL), F32),
        scratch_shapes=[pltpu.VMEM((GDN_HEADS // 2, GDN_DK, 2 * GDN_DV), F32),
                        pltpu.VMEM((R + 8, GDN_QKV), F32)],
        compiler_params=_cparams(("arbitrary", "arbitrary")),
        name="gdn",
    )(proj, gates, conv_w, alog_p, dtb_p)


def _merge_out_kernel(o_ref, yr_ref, tail_ref, gng_ref, w_ref, x_ref, g_ref, xo_ref, hn_ref):
    DV = GDN_DV
    gng = gng_ref[...]

    def tail_cols(off):
        return tail_ref[:, off - OFF_GZ:off - OFF_GZ + LANE]

    parts = []
    for h in range(GDN_HEADS):
        hs = slice(h * DV, (h + 1) * DV)
        o = o_ref[:, hs]
        on = o * lax.rsqrt(jnp.mean(o * o, axis=-1, keepdims=True) + NORM_EPS) * gng
        yg = _silu(tail_cols(OFF_GZ + h * DV)) * on
        merged = (_sigmoid(tail_cols(OFF_GATE_A + h * DV)) * yr_ref[:, hs]
                  + _sigmoid(tail_cols(OFF_GATE_B + h * DV)) * yg)
        parts.append(merged.astype(BF16))
    x = _dot(jnp.concatenate(parts, axis=1), w_ref[...]) + x_ref[...]
    xo_ref[...] = x
    ms = jnp.mean(x * x, axis=-1, keepdims=True)
    hn_ref[...] = (x * lax.rsqrt(ms + NORM_EPS) * g_ref[...]).astype(hn_ref.dtype)


def _merge_out(o_gdn, yr, tail, gdn_norm_g, w_out_b, layer, x2, next_g, tm=256):
    t, d = x2.shape
    assert t % tm == 0 and w_out_b.dtype == BF16 and tail.shape == (t, TAIL_WIDTH)

    def rows(w):
        return pl.BlockSpec((tm, w), lambda i: (i, 0))

    return pl.pallas_call(
        _merge_out_kernel,
        grid=(t // tm,),
        in_specs=[rows(d), rows(d), rows(TAIL_WIDTH),
                  pl.BlockSpec((1, GDN_DV), lambda i: (0, 0)),
                  pl.BlockSpec((None, d, d), lambda i: (layer, 0, 0)),
                  rows(d),
                  pl.BlockSpec((1, d), lambda i: (0, 0))],
        out_specs=[rows(d), rows(d)],
        out_shape=[jax.ShapeDtypeStruct((t, d), F32), jax.ShapeDtypeStruct((t, d), BF16)],
        compiler_params=_cparams(("parallel",)),
        name="merge_out",
    )(o_gdn, yr, tail, gdn_norm_g.reshape(1, GDN_DV), w_out_b, x2, next_g.reshape(1, d))


def _xa_fold_q_kernel(wq_ref, k_ref, o_ref):
    a = _dot_nt(wq_ref[...].astype(BF16), k_ref[...]) * (XATTN_DH ** -0.5)
    o_ref[...] = a.astype(o_ref.dtype)


def _xa_fold_o_kernel(v_ref, wo_ref, o_ref):
    o_ref[...] = _dot(v_ref[...], wo_ref[...].astype(BF16)).astype(o_ref.dtype)


def _xa_main_kernel(h_ref, a_ref, b_ref, r_ref, g_ref, o_ref, hn_ref, *, mem_tokens):
    s = _dot(h_ref[...], a_ref[...])
    ps = []
    for hh in range(XATTN_HEADS):
        seg = s[:, hh * mem_tokens:(hh + 1) * mem_tokens]
        p = jnp.exp(seg - jnp.max(seg, axis=-1, keepdims=True))
        ps.append((p / jnp.sum(p, axis=-1, keepdims=True)).astype(BF16))
    x = _dot(jnp.concatenate(ps, axis=1), b_ref[...]) + r_ref[...]
    o_ref[...] = x
    ms = jnp.mean(x * x, axis=-1, keepdims=True)
    hn_ref[...] = (x * lax.rsqrt(ms + NORM_EPS) * g_ref[...]).astype(hn_ref.dtype)


def _xattn_residual(h, kv, w_xq, w_xo, layer, x2, next_g, batch, seq, mem_tokens, tq=512):
    t, d = h.shape
    hm = XATTN_HEADS * mem_tokens
    nq = seq // tq
    grid_bh = (batch, XATTN_HEADS)
    fold_q = pl.pallas_call(
        _xa_fold_q_kernel,
        grid=grid_bh,
        in_specs=[pl.BlockSpec((None, d, XATTN_DH), lambda b, hh: (layer, 0, hh)),
                  pl.BlockSpec((mem_tokens, XATTN_DH), lambda b, hh: (b, hh))],
        out_specs=pl.BlockSpec((None, d, mem_tokens), lambda b, hh: (b, 0, hh)),
        out_shape=jax.ShapeDtypeStruct((batch, d, hm), BF16),
        compiler_params=_cparams(("parallel", "parallel")),
        name="xattn_fold_q",
    )(w_xq, kv)
    fold_o = pl.pallas_call(
        _xa_fold_o_kernel,
        grid=grid_bh,
        in_specs=[pl.BlockSpec((mem_tokens, XATTN_DH), lambda b, hh: (b, XATTN_HEADS + hh)),
                  pl.BlockSpec((None, XATTN_DH, d), lambda b, hh: (layer, hh, 0))],
        out_specs=pl.BlockSpec((None, mem_tokens, d), lambda b, hh: (b, hh, 0)),
        out_shape=jax.ShapeDtypeStruct((batch, hm, d), BF16),
        compiler_params=_cparams(("parallel", "parallel")),
        name="xattn_fold_o",
    )(kv, w_xo)
    return pl.pallas_call(
        functools.partial(_xa_main_kernel, mem_tokens=mem_tokens),
        grid=(batch, nq),
        in_specs=[pl.BlockSpec((tq, d), lambda b, i: (b * nq + i, 0)),
                  pl.BlockSpec((None, d, hm), lambda b, i: (b, 0, 0)),
                  pl.BlockSpec((None, hm, d), lambda b, i: (b, 0, 0)),
                  pl.BlockSpec((tq, d), lambda b, i: (b * nq + i, 0)),
                  pl.BlockSpec((1, d), lambda b, i: (0, 0))],
        out_specs=[pl.BlockSpec((tq, d), lambda b, i: (b * nq + i, 0))] * 2,
        out_shape=[jax.ShapeDtypeStruct((t, d), F32), jax.ShapeDtypeStruct((t, d), BF16)],
        compiler_params=_cparams(("parallel", "parallel")),
        name="xattn",
    )(h, fold_q, fold_o, x2, next_g.reshape(1, d))


def kernel(x, mem, positions, norm_mix_g, w_in, conv_w, gdn_a_log, gdn_dt_bias, ret_gn_g, gdn_norm_g, w_out, norm_x_g, norm_mem_g, w_xq, w_xkv, w_xo, norm_ffn_g, w_ff1, w_ff2, norm_final_g):
    batch, seq, d = x.shape
    mem_tokens = mem.shape[1]
    depth = w_in.shape[0]
    t = batch * seq
    x2 = x.reshape(t, d)
    mem2 = mem.reshape(batch * mem_tokens, d)

    cos2, sin2 = _rope_tables(positions)
    ret_tables = _retention_tables()
    w_ff2_b = w_ff2.astype(BF16)
    w_out_b = w_out.astype(BF16)
    w_in_nk = jnp.swapaxes(w_in, 1, 2)

    for l in range(depth):
        h = _rmsnorm(x2, norm_mix_g[l], BF16)
        proj = _matmul(h, w_in_nk, l, OFF_GA, F32, tm=1024, tn=PROJ_TN, w_is_nk=True)
        gates = _matmul(h, w_in_nk, l, LANE, F32, tm=1024, tn=LANE, col0=OFF_GA, w_is_nk=True)
        tail = _matmul(h, w_in_nk, l, TAIL_WIDTH, F32, tm=1024, tn=PROJ_TN, col0=OFF_GZ, w_is_nk=True)
        yr = _retention(proj, cos2, sin2, ret_tables, ret_gn_g[l], batch, seq)
        o_gdn = _gdn(proj, gates, conv_w[l], gdn_a_log[l], gdn_dt_bias[l], batch, seq)
        x2, h = _merge_out(o_gdn, yr, tail, gdn_norm_g[l], w_out_b, l, x2, norm_x_g[l])
        mem_n = _rmsnorm(mem2, norm_mem_g[l], BF16)
        kv = _matmul(mem_n, w_xkv, l, 2 * d, BF16, tm=batch * mem_tokens, tn=1024)
        x2, h = _xattn_residual(h, kv, w_xq, w_xo, l, x2, norm_ffn_g[l], batch, seq, mem_tokens)
        hid = _matmul(h, w_ff1, l, D_FF, BF16, tm=1024, tn=1024, relu2=True)
        x2 = _matmul(hid, w_ff2_b, l, d, F32, tm=512, tn=512, residual=x2)

    out = _rmsnorm(x2, norm_final_g, x.dtype)
    return out.reshape(batch, seq, d)
```

```python
import functools

import numpy as np
import jax
import jax.numpy as jnp
from jax import lax
from jax.experimental import pallas as pl
from jax.experimental.pallas import tpu as pltpu

F32 = jnp.float32
BF16 = jnp.bfloat16

D_MODEL = 2048
CHUNK = 64
RET_HEADS = 8
RET_DK = D_MODEL // 16
RET_DV = D_MODEL // RET_HEADS
GDN_HEADS = 16
GDN_DK = D_MODEL // GDN_HEADS
GDN_DV = D_MODEL // GDN_HEADS
CONV_K = 4
XATTN_HEADS = 4
XATTN_DH = D_MODEL // XATTN_HEADS
D_FF = 4 * D_MODEL
ROPE_THETA = 10000.0
NORM_EPS = 1e-6
GDN_QKV = 3 * GDN_HEADS * GDN_DK
OFF_RQ = 0
OFF_RK = OFF_RQ + RET_HEADS * RET_DK
OFF_RV = OFF_RK + RET_HEADS * RET_DK
OFF_RG = OFF_RV + RET_HEADS * RET_DV
OFF_GQKV = OFF_RG + RET_HEADS * RET_DV
OFF_GA = OFF_GQKV + GDN_QKV
OFF_GB = OFF_GA + GDN_HEADS
OFF_GZ = OFF_GB + GDN_HEADS
OFF_GATE_A = OFF_GZ + GDN_HEADS * GDN_DV
OFF_GATE_B = OFF_GATE_A + D_MODEL
IN_WIDTH = OFF_GATE_B + D_MODEL

LANE = 128
PROJ_TN = 1024
TAIL_WIDTH = IN_WIDTH - OFF_GZ
RET_BLOCK = 256
GDN_STEP_CHUNKS = 2
VMEM_LIMIT = 56 * 1024 * 1024


def _cparams(sem):
    return pltpu.CompilerParams(dimension_semantics=sem, vmem_limit_bytes=VMEM_LIMIT)


def _dot(a, b):
    return jnp.dot(a, b, preferred_element_type=F32)


def _dot_nt(a, b):
    return lax.dot_general(a, b, (((1,), (1,)), ((), ())), preferred_element_type=F32)


def _dot_tn(a, b):
    return lax.dot_general(a, b, (((0,), (0,)), ((), ())), preferred_element_type=F32)


NEG_LOG2E = -1.4426950408889634


def _sigmoid(x):
    return 1.0 / (1.0 + jnp.exp2(x * NEG_LOG2E))


def _silu(x):
    return x * _sigmoid(x)


def _rmsnorm_kernel(x_ref, g_ref, o_ref):
    x = x_ref[...]
    ms = jnp.mean(x * x, axis=-1, keepdims=True)
    o_ref[...] = (x * lax.rsqrt(ms + NORM_EPS) * g_ref[...]).astype(o_ref.dtype)


def _rmsnorm(x, g, out_dtype, tm=512):
    m, d = x.shape
    tm = min(tm, m)
    return pl.pallas_call(
        _rmsnorm_kernel,
        grid=(m // tm,),
        in_specs=[pl.BlockSpec((tm, d), lambda i: (i, 0)),
                  pl.BlockSpec((1, d), lambda i: (0, 0))],
        out_specs=pl.BlockSpec((tm, d), lambda i: (i, 0)),
        out_shape=jax.ShapeDtypeStruct((m, d), out_dtype),
        compiler_params=_cparams(("parallel",)),
        name="rmsnorm",
    )(x, g.reshape(1, d))


def _mm_kernel(*refs, relu2, has_res, cast_w, w_is_nk):
    it = iter(refs)
    a_ref = next(it)
    w_ref = next(it)
    r_ref = next(it) if has_res else None
    o_ref = next(it)
    wbf_ref = next(it) if cast_w else w_ref

    if cast_w:
        @pl.when(pl.program_id(1) == 0)
        def _():
            if w_is_nk:
                w = w_ref[0].T
            else:
                w = w_ref[...]
            wbf_ref[...] = w.astype(BF16)

    acc = _dot(a_ref[...], wbf_ref[...])
    if relu2:
        acc = jnp.square(jnp.maximum(acc, 0.0))
    if has_res:
        acc = acc + r_ref[...]
    o_ref[...] = acc.astype(o_ref.dtype)


def _matmul(a, w, layer, n_out, out_dtype, tm, tn, relu2=False, residual=None, col0=0,
            w_is_nk=False):
    m, k = a.shape
    assert w.shape[2 if w_is_nk else 1] == k and n_out % tn == 0 and m % tm == 0
    assert col0 + n_out <= w.shape[1 if w_is_nk else 2]
    cast_w = w.dtype != BF16 or w_is_nk
    if w_is_nk:
        assert col0 % 8 == 0
        w_spec = pl.BlockSpec((pl.Element(1), pl.Element(tn), pl.Element(k)),
                              lambda j, i: (layer, pl.multiple_of(col0 + j * tn, 8), 0))
    else:
        assert col0 % tn == 0
        w_spec = pl.BlockSpec((None, k, tn), lambda j, i: (layer, 0, col0 // tn + j))
    in_specs = [pl.BlockSpec((tm, k), lambda j, i: (i, 0)), w_spec]
    args = [a, w]
    if residual is not None:
        in_specs.append(pl.BlockSpec((tm, tn), lambda j, i: (i, j)))
        args.append(residual)
    return pl.pallas_call(
        functools.partial(_mm_kernel, relu2=relu2, has_res=residual is not None, cast_w=cast_w,
                          w_is_nk=w_is_nk),
        grid=(n_out // tn, m // tm),
        in_specs=in_specs,
        out_specs=pl.BlockSpec((tm, tn), lambda j, i: (i, j)),
        out_shape=jax.ShapeDtypeStruct((m, n_out), out_dtype),
        scratch_shapes=[pltpu.VMEM((k, tn), BF16)] if cast_w else [],
        compiler_params=_cparams(("arbitrary", "arbitrary")),
        name="matmul",
    )(*args)


def _rope_kernel(pos_ref, invf_ref, sgn_ref, cos_ref, sin_ref):
    ang = pos_ref[...] * invf_ref[...]
    cos_ref[...] = jnp.cos(ang)
    sin_ref[...] = jnp.sin(ang) * sgn_ref[...]


def _rope_tables(positions):
    t = positions.size
    half = RET_DK // 2
    inv_freq = 1.0 / (ROPE_THETA ** (jnp.arange(0, RET_DK, 2, dtype=F32) / RET_DK))
    invf2 = jnp.concatenate([inv_freq, inv_freq]).reshape(1, RET_DK)
    sgn = jnp.concatenate([-jnp.ones((half,), F32), jnp.ones((half,), F32)]).reshape(1, RET_DK)
    pos = positions.astype(F32).reshape(t, 1)
    tm = 1024
    return pl.pallas_call(
        _rope_kernel,
        grid=(t // tm,),
        in_specs=[pl.BlockSpec((tm, 1), lambda i: (i, 0)),
                  pl.BlockSpec((1, RET_DK), lambda i: (0, 0)),
                  pl.BlockSpec((1, RET_DK), lambda i: (0, 0))],
        out_specs=[pl.BlockSpec((tm, RET_DK), lambda i: (i, 0))] * 2,
        out_shape=[jax.ShapeDtypeStruct((t, RET_DK), F32)] * 2,
        compiler_params=_cparams(("parallel",)),
        name="rope_tables",
    )(pos, invf2, sgn)


def _retention_tables():
    idx = np.arange(RET_BLOCK, dtype=np.float64)
    log_gamma = jnp.log1p(-jnp.exp2(-5.0 - jnp.arange(RET_HEADS, dtype=F32)))
    lg = log_gamma[:, None, None]
    dist = jnp.asarray(np.abs(idx[:, None] - idx[None, :]), F32)
    visible = jnp.asarray((idx[None, :] // CHUNK) <= (idx[:, None] // CHUNK))
    wmat = jnp.where(visible[None], jnp.exp(lg * dist[None]), 0.0)
    qd = jnp.exp(log_gamma[:, None] * jnp.asarray(idx + 1.0, F32))
    kd = jnp.exp(log_gamma[:, None] * jnp.asarray(RET_BLOCK - 1.0 - idx, F32))
    cd = jnp.exp(log_gamma * RET_BLOCK)
    qd = jnp.broadcast_to(qd[:, :, None], (RET_HEADS, RET_BLOCK, RET_DK))
    kd = jnp.broadcast_to(kd[:, :, None], (RET_HEADS, RET_BLOCK, RET_DK))
    cd = jnp.broadcast_to(cd[:, None, None], (RET_HEADS, 8, RET_DV))
    return wmat, qd, kd, cd


def _retention_kernel(q_ref, k_ref, v_ref, rg_ref, cos_ref, sin_ref, w_ref, qd_ref, kd_ref,
                      cd_ref, gn_ref, o_ref, state_ref):
    @pl.when(pl.program_id(1) == 0)
    def _():
        state_ref[...] = jnp.zeros_like(state_ref)

    cos = cos_ref[...]
    sin = sin_ref[...]
    half = RET_DK // 2
    heads = range(RET_HEADS)
    qr, kr, vb = [], [], []
    for h in heads:
        ks = slice(h * RET_DK, (h + 1) * RET_DK)
        q = q_ref[:, ks]
        k = k_ref[:, ks]
        qr.append(q * cos + pltpu.roll(q, half, axis=1) * sin)
        kr.append((k * cos + pltpu.roll(k, half, axis=1) * sin) * (RET_DK ** -0.5))
        vb.append(v_ref[:, h * RET_DV:(h + 1) * RET_DV].astype(BF16))
    scores = [_dot_nt(qr[h].astype(BF16), kr[h].astype(BF16)) * w_ref[h] for h in heads]
    s_prev = [state_ref[h] for h in heads]
    o = [_dot(scores[h].astype(BF16), vb[h])
         + _dot((qr[h] * qd_ref[h]).astype(BF16), s_prev[h].astype(BF16)) for h in heads]
    for h in heads:
        state_ref[h] = (s_prev[h] * cd_ref[h][0:1, :]
                        + _dot_tn((kr[h] * kd_ref[h]).astype(BF16), vb[h]))
    for h in heads:
        vs = slice(h * RET_DV, (h + 1) * RET_DV)
        mu = jnp.mean(o[h], axis=-1, keepdims=True)
        oc = o[h] - mu
        var = jnp.mean(oc * oc, axis=-1, keepdims=True)
        y = oc * lax.rsqrt(var + 1e-5) * gn_ref[:, vs]
        o_ref[:, vs] = (_silu(rg_ref[:, vs]) * y).astype(o_ref.dtype)


def _retention(proj, cos2, sin2, tables, gn_g, batch, seq):
    wmat, qd, kd, cd = tables
    t = batch * seq
    nb = seq // RET_BLOCK
    L = RET_BLOCK
    qk_w = RET_HEADS * RET_DK
    v_w = RET_HEADS * RET_DV
    assert OFF_RQ % qk_w == 0 and OFF_RK % qk_w == 0 and OFF_RV % v_w == 0 and OFF_RG % v_w == 0

    def row(b, s):
        return b * nb + s

    def whole(x):
        return pl.BlockSpec(x.shape, lambda b, s: (0,) * x.ndim)

    gn2 = gn_g.reshape(1, v_w)
    return pl.pallas_call(
        _retention_kernel,
        grid=(batch, nb),
        in_specs=[
            pl.BlockSpec((L, qk_w), lambda b, s: (row(b, s), OFF_RQ // qk_w)),
            pl.BlockSpec((L, qk_w), lambda b, s: (row(b, s), OFF_RK // qk_w)),
            pl.BlockSpec((L, v_w), lambda b, s: (row(b, s), OFF_RV // v_w)),
            pl.BlockSpec((L, v_w), lambda b, s: (row(b, s), OFF_RG // v_w)),
            pl.BlockSpec((L, RET_DK), lambda b, s: (row(b, s), 0)),
            pl.BlockSpec((L, RET_DK), lambda b, s: (row(b, s), 0)),
            whole(wmat), whole(qd), whole(kd), whole(cd), whole(gn2),
        ],
        out_specs=pl.BlockSpec((L, v_w), lambda b, s: (row(b, s), 0)),
        out_shape=jax.ShapeDtypeStruct((t, v_w), BF16),
        scratch_shapes=[pltpu.VMEM((RET_HEADS, RET_DK, RET_DV), F32)],
        compiler_params=_cparams(("arbitrary", "arbitrary")),
        name="retention",
    )(proj, proj, proj, proj, cos2, sin2, wmat, qd, kd, cd, gn2)


def _cumsum_rows(x):
    row = lax.broadcasted_iota(jnp.int32, x.shape, 0)
    s = 1
    while s < x.shape[0]:
        x = x + jnp.where(row >= s, pltpu.roll(x, s, axis=0), 0.0)
        s *= 2
    return x


def _softplus(x):
    return jnp.maximum(x, 0.0) + jnp.log1p(jnp.exp(-jnp.abs(x)))


def _gdn_kernel(qkv_ref, gates_ref, convw_ref, alog_ref, dtb_ref, o_ref, state_ref, xext_ref):
    C = CHUNK
    HALO = 8
    NB = GDN_STEP_CHUNKS
    NP = GDN_HEADS // 2
    DK, DV = GDN_DK, GDN_DV

    @pl.when(pl.program_id(1) == 0)
    def _():
        state_ref[...] = jnp.zeros_like(state_ref)
        xext_ref[0:HALO, :] = jnp.zeros((HALO, GDN_QKV), F32)

    xext_ref[HALO:HALO + NB * C, :] = qkv_ref[...]

    row = lax.broadcasted_iota(jnp.int32, (C, 2 * C), 0)
    lane = lax.broadcasted_iota(jnp.int32, (C, 2 * C), 1)
    col = lane & (C - 1)
    lo = lane < C
    lo_t = lax.broadcasted_iota(jnp.int32, (2 * C, 2 * C), 1) < C
    causal = row >= col
    strict = row > col
    lvl_masks = []
    for b in range(C.bit_length() - 1):
        ib = lax.shift_right_logical(row, b)
        jb = lax.shift_right_logical(col, b)
        lvl_masks.append(((ib & 1) == 1) & (jb == ib - 1))

    def conv_act(ci, col0):
        sl = slice(col0, col0 + LANE)
        r0 = HALO + ci * C
        y = convw_ref[CONV_K - 1:CONV_K, sl] * xext_ref[r0:r0 + C, sl]
        for d in range(1, CONV_K):
            y = y + convw_ref[CONV_K - 1 - d:CONV_K - d, sl] * xext_ref[r0 - d:r0 - d + C, sl]
        return _silu(y)

    def block_diag(x, mask):
        return jnp.concatenate([jnp.where(mask, x, 0.0), jnp.where(mask, 0.0, x)], axis=0).astype(BF16)

    def block_diag_wide(xa, xb):
        za = jnp.zeros_like(xa)
        top = jnp.concatenate([xa, za], axis=1)
        bot = jnp.concatenate([za, xb], axis=1)
        return jnp.concatenate([top, bot], axis=0).astype(BF16)

    def bcast_col(x, j):
        return jnp.broadcast_to(x[:, j:j + 1], (C, LANE))

    chunks = range(NB)
    heads = range(GDN_HEADS)
    items = [(ci, p) for ci in chunks for p in range(NP)]

    cg, cg_t, bt = [], [], []
    for ci in chunks:
        sm = gates_ref[ci * C:(ci + 1) * C, :]
        gl = -jnp.exp(alog_ref[...]) * _softplus(sm + dtb_ref[...])
        cg.append(_cumsum_rows(gl))
        cg_t.append(cg[ci].T)
        bt.append(_sigmoid(sm))

    qn, kn, vv, cgc, btc, ecg, kb = {}, {}, {}, {}, {}, {}, {}
    for ci in chunks:
        for h in heads:
            q = conv_act(ci, h * DK)
            k = conv_act(ci, GDN_HEADS * DK + h * DK)
            vv[ci, h] = conv_act(ci, 2 * GDN_HEADS * DK + h * DV)
            qn[ci, h] = q * lax.rsqrt(jnp.sum(q * q, axis=-1, keepdims=True) + 1e-6) * (DK ** -0.5)
            kn[ci, h] = k * lax.rsqrt(jnp.sum(k * k, axis=-1, keepdims=True) + 1e-6)
    for ci in chunks:
        for h in heads:
            cgc[ci, h] = bcast_col(cg[ci], h)
            btc[ci, h] = bcast_col(bt[ci], GDN_HEADS + h)
            ecg[ci, h] = jnp.exp(cgc[ci, h])
            kb[ci, h] = kn[ci, h] * btc[ci, h]

    decay, knt, m, qk = {}, {}, {}, {}
    for it in items:
        ci, p = it
        a, b = (ci, 2 * p), (ci, 2 * p + 1)
        cgc_p = jnp.where(lo, cgc[a], cgc[b])
        cgr_p = jnp.concatenate([cg_t[ci][2 * p:2 * p + 1, :], cg_t[ci][2 * p + 1:2 * p + 2, :]], axis=1)
        decay[it] = jnp.where(causal, jnp.exp(jnp.where(causal, cgc_p - cgr_p, 0.0)), 0.0)
        knt[it] = jnp.concatenate([kn[a], kn[b]], axis=0).T
    for it in items:
        ci, p = it
        a, b = (ci, 2 * p), (ci, 2 * p + 1)
        lhs = jnp.concatenate([jnp.concatenate([kb[a], kb[b]], axis=1),
                               jnp.concatenate([qn[a], qn[b]], axis=1)], axis=0)
        kq = _dot(lhs.astype(BF16), block_diag(knt[it], lo_t))
        m[it] = jnp.where(strict, kq[:C] * decay[it], 0.0)
        qk[it] = kq[C:] * decay[it]

    r = {it: jnp.where(lvl_masks[0], -m[it], 0.0) for it in items}
    for lm in lvl_masks[1:]:
        cm = {it: jnp.where(lm, m[it], 0.0) for it in items}
        y = {it: cm[it] + _dot(r[it].astype(BF16), block_diag(cm[it], lo)) for it in items}
        r = {it: r[it] - (y[it] + _dot(y[it].astype(BF16), block_diag(r[it], lo))) for it in items}

    u, w = {}, {}
    for it in items:
        ci, p = it
        a, b = (ci, 2 * p), (ci, 2 * p + 1)
        xa = jnp.concatenate([vv[a] * btc[a], kb[a] * ecg[a]], axis=1)
        xb = jnp.concatenate([vv[b] * btc[b], kb[b] * ecg[b]], axis=1)
        rx = _dot(r[it].astype(BF16), block_diag_wide(xa, xb))
        u[it] = (jnp.concatenate([xa[:, :DV], xb[:, :DV]], axis=1) +
                 jnp.concatenate([rx[:, :DV], rx[:, DV + DK:2 * DV + DK]], axis=1))
        w[it] = (jnp.concatenate([xa[:, DV:], xb[:, DV:]], axis=1) +
                 jnp.concatenate([rx[:, DV:DV + DK], rx[:, 2 * DV + DK:]], axis=1))

    state = [state_ref[p] for p in range(NP)]
    for ci in chunks:
        wsqs, v_new = {}, {}
        for p in range(NP):
            it = (ci, p)
            a, b = (ci, 2 * p), (ci, 2 * p + 1)
            qe = jnp.concatenate([qn[a] * ecg[a], qn[b] * ecg[b]], axis=1)
            lhs = jnp.concatenate([w[it], qe], axis=0)
            wsqs[p] = _dot(lhs.astype(BF16), block_diag_wide(state[p][:, :DV], state[p][:, DV:]))
            v_new[p] = u[it] - wsqs[p][:C]
        for p in range(NP):
            it = (ci, p)
            a, b = (ci, 2 * p), (ci, 2 * p + 1)
            g_last_a = cgc[a][C - 1:C, :]
            g_last_b = cgc[b][C - 1:C, :]
            edr = jnp.exp(jnp.concatenate([g_last_a[:, :C] - cg_t[ci][2 * p:2 * p + 1, :],
                                           g_last_b[:, :C] - cg_t[ci][2 * p + 1:2 * p + 2, :]], axis=1))
            lhs = jnp.concatenate([qk[it], knt[it] * edr], axis=0)
            res = _dot(lhs.astype(BF16), block_diag_wide(v_new[p][:, :DV], v_new[p][:, DV:]))
            eg = jnp.exp(jnp.concatenate([g_last_a, g_last_b], axis=1))
            state[p] = state[p] * eg + res[C:]
            o_ref[ci * C:(ci + 1) * C, 2 * p * DV:(2 * p + 2) * DV] = (wsqs[p][C:] + res[:C]).astype(o_ref.dtype)
    for p in range(NP):
        state_ref[p] = state[p]

    xext_ref[0:HALO, :] = xext_ref[NB * C:NB * C + HALO, :]


def _gdn(proj, gates, conv_w, a_log, dt_bias, batch, seq):
    t = batch * seq
    R = GDN_STEP_CHUNKS * CHUNK
    ns = seq // R
    pad = LANE - GDN_HEADS
    alog_p = jnp.pad(a_log.astype(F32), (0, pad)).reshape(1, LANE)
    dtb_p = jnp.pad(dt_bias.astype(F32), (0, pad)).reshape(1, LANE)
    assert OFF_GQKV % GDN_QKV == 0 and seq % R == 0

    def row(b, c):
        return b * ns + c

    return pl.pallas_call(
        _gdn_kernel,
        grid=(batch, ns),
        in_specs=[
            pl.BlockSpec((R, GDN_QKV), lambda b, c: (row(b, c), OFF_GQKV // GDN_QKV)),
            pl.BlockSpec((R, LANE), lambda b, c: (row(b, c), 0)),
            pl.BlockSpec((CONV_K, GDN_QKV), lambda b, c: (0, 0)),
            pl.BlockSpec((1, LANE), lambda b, c: (0, 0)),
            pl.BlockSpec((1, LANE), lambda b, c: (0, 0)),
        ],
        out_specs=pl.BlockSpec((R, D_MODEL), lambda b, c: (row(b, c), 0)),
        out_shape=jax.ShapeDtypeStruct((t, D_MODEL), BF16),
        scratch_shapes=[pltpu.VMEM((GDN_HEADS // 2, GDN_DK, 2 * GDN_DV), F32),
                        pltpu.VMEM((R + 8, GDN_QKV), F32)],
        compiler_params=_cparams(("arbitrary", "arbitrary")),
        name="gdn",
    )(proj, gates, conv_w, alog_p, dtb_p)


def _merge_out_kernel(o_ref, yr_ref, tail_ref, gng_ref, w_ref, x_ref, g_ref, xo_ref, hn_ref):
    DV = GDN_DV
    gng = gng_ref[...]

    def tail_cols(off):
        return tail_ref[:, off - OFF_GZ:off - OFF_GZ + LANE].astype(F32)

    parts = []
    for h in range(GDN_HEADS):
        hs = slice(h * DV, (h + 1) * DV)
        o = o_ref[:, hs].astype(F32)
        on = o * lax.rsqrt(jnp.mean(o * o, axis=-1, keepdims=True) + NORM_EPS) * gng
        yg = _silu(tail_cols(OFF_GZ + h * DV)) * on
        merged = (_sigmoid(tail_cols(OFF_GATE_A + h * DV)) * yr_ref[:, hs].astype(F32)
                  + _sigmoid(tail_cols(OFF_GATE_B + h * DV)) * yg)
        parts.append(merged.astype(BF16))
    x = _dot(jnp.concatenate(parts, axis=1), w_ref[...]) + x_ref[...]
    xo_ref[...] = x
    ms = jnp.mean(x * x, axis=-1, keepdims=True)
    hn_ref[...] = (x * lax.rsqrt(ms + NORM_EPS) * g_ref[...]).astype(hn_ref.dtype)


def _merge_out(o_gdn, yr, tail, gdn_norm_g, w_out_b, layer, x2, next_g, tm=256):
    t, d = x2.shape
    assert t % tm == 0 and w_out_b.dtype == BF16 and tail.shape == (t, TAIL_WIDTH)

    def rows(w):
        return pl.BlockSpec((tm, w), lambda i: (i, 0))

    return pl.pallas_call(
        _merge_out_kernel,
        grid=(t // tm,),
        in_specs=[rows(d), rows(d), rows(TAIL_WIDTH),
                  pl.BlockSpec((1, GDN_DV), lambda i: (0, 0)),
                  pl.BlockSpec((None, d, d), lambda i: (layer, 0, 0)),
                  rows(d),
                  pl.BlockSpec((1, d), lambda i: (0, 0))],
        out_specs=[rows(d), rows(d)],
        out_shape=[jax.ShapeDtypeStruct((t, d), F32), jax.ShapeDtypeStruct((t, d), BF16)],
        compiler_params=_cparams(("parallel",)),
        name="merge_out",
    )(o_gdn, yr, tail, gdn_norm_g.reshape(1, GDN_DV), w_out_b, x2, next_g.reshape(1, d))


def _xa_fold_q_kernel(wq_ref, k_ref, o_ref):
    a = _dot_nt(wq_ref[...].astype(BF16), k_ref[...]) * (XATTN_DH ** -0.5)
    o_ref[...] = a.astype(o_ref.dtype)


def _xa_fold_o_kernel(v_ref, wo_ref, o_ref):
    o_ref[...] = _dot(v_ref[...], wo_ref[...].astype(BF16)).astype(o_ref.dtype)


def _xa_main_kernel(h_ref, a_ref, b_ref, r_ref, g_ref, o_ref, hn_ref, *, mem_tokens):
    s = _dot(h_ref[...], a_ref[...])
    ps = []
    for hh in range(XATTN_HEADS):
        seg = s[:, hh * mem_tokens:(hh + 1) * mem_tokens]
        p = jnp.exp(seg - jnp.max(seg, axis=-1, keepdims=True))
        ps.append((p / jnp.sum(p, axis=-1, keepdims=True)).astype(BF16))
    x = _dot(jnp.concatenate(ps, axis=1), b_ref[...]) + r_ref[...]
    o_ref[...] = x
    ms = jnp.mean(x * x, axis=-1, keepdims=True)
    hn_ref[...] = (x * lax.rsqrt(ms + NORM_EPS) * g_ref[...]).astype(hn_ref.dtype)


def _xattn_residual(h, kv, w_xq, w_xo, layer, x2, next_g, batch, seq, mem_tokens, tq=512):
    t, d = h.shape
    hm = XATTN_HEADS * mem_tokens
    nq = seq // tq
    grid_bh = (batch, XATTN_HEADS)
    fold_q = pl.pallas_call(
        _xa_fold_q_kernel,
        grid=grid_bh,
        in_specs=[pl.BlockSpec((None, d, XATTN_DH), lambda b, hh: (layer, 0, hh)),
                  pl.BlockSpec((mem_tokens, XATTN_DH), lambda b, hh: (b, hh))],
        out_specs=pl.BlockSpec((None, d, mem_tokens), lambda b, hh: (b, 0, hh)),
        out_shape=jax.ShapeDtypeStruct((batch, d, hm), BF16),
        compiler_params=_cparams(("parallel", "parallel")),
        name="xattn_fold_q",
    )(w_xq, kv)
    fold_o = pl.pallas_call(
        _xa_fold_o_kernel,
        grid=grid_bh,
        in_specs=[pl.BlockSpec((mem_tokens, XATTN_DH), lambda b, hh: (b, XATTN_HEADS + hh)),
                  pl.BlockSpec((None, XATTN_DH, d), lambda b, hh: (layer, hh, 0))],
        out_specs=pl.BlockSpec((None, mem_tokens, d), lambda b, hh: (b, hh, 0)),
        out_shape=jax.ShapeDtypeStruct((batch, hm, d), BF16),
        compiler_params=_cparams(("parallel", "parallel")),
        name="xattn_fold_o",
    )(kv, w_xo)
    return pl.pallas_call(
        functools.partial(_xa_main_kernel, mem_tokens=mem_tokens),
        grid=(batch, nq),
        in_specs=[pl.BlockSpec((tq, d), lambda b, i: (b * nq + i, 0)),
                  pl.BlockSpec((None, d, hm), lambda b, i: (b, 0, 0)),
                  pl.BlockSpec((None, hm, d), lambda b, i: (b, 0, 0)),
                  pl.BlockSpec((tq, d), lambda b, i: (b * nq + i, 0)),
                  pl.BlockSpec((1, d), lambda b, i: (0, 0))],
        out_specs=[pl.BlockSpec((tq, d), lambda b, i: (b * nq + i, 0))] * 2,
        out_shape=[jax.ShapeDtypeStruct((t, d), F32), jax.ShapeDtypeStruct((t, d), BF16)],
        compiler_params=_cparams(("parallel", "parallel")),
        name="xattn",
    )(h, fold_q, fold_o, x2, next_g.reshape(1, d))


def kernel(x, mem, positions, norm_mix_g, w_in, conv_w, gdn_a_log, gdn_dt_bias, ret_gn_g, gdn_norm_g, w_out, norm_x_g, norm_mem_g, w_xq, w_xkv, w_xo, norm_ffn_g, w_ff1, w_ff2, norm_final_g):
    batch, seq, d = x.shape
    mem_tokens = mem.shape[1]
    depth = w_in.shape[0]
    t = batch * seq
    x2 = x.reshape(t, d)
    mem2 = mem.reshape(batch * mem_tokens, d)

    cos2, sin2 = _rope_tables(positions)
    ret_tables = _retention_tables()
    w_ff2_b = w_ff2.astype(BF16)
    w_out_b = w_out.astype(BF16)
    w_in_nk = jnp.swapaxes(w_in, 1, 2)

    for l in range(depth):
        h = _rmsnorm(x2, norm_mix_g[l], BF16)
        proj = _matmul(h, w_in_nk, l, OFF_GA, F32, tm=1024, tn=PROJ_TN, w_is_nk=True)
        gates = _matmul(h, w_in_nk, l, LANE, F32, tm=1024, tn=LANE, col0=OFF_GA, w_is_nk=True)
        tail = _matmul(h, w_in_nk, l, TAIL_WIDTH, BF16, tm=1024, tn=PROJ_TN, col0=OFF_GZ, w_is_nk=True)
        yr = _retention(proj, cos2, sin2, ret_tables, ret_gn_g[l], batch, seq)
        o_gdn = _gdn(proj, gates, conv_w[l], gdn_a_log[l], gdn_dt_bias[l], batch, seq)
        x2, h = _merge_out(o_gdn, yr, tail, gdn_norm_g[l], w_out_b, l, x2, norm_x_g[l])
        mem_n = _rmsnorm(mem2, norm_mem_g[l], BF16)
        kv = _matmul(mem_n, w_xkv, l, 2 * d, BF16, tm=batch * mem_tokens, tn=1024)
        x2, h = _xattn_residual(h, kv, w_xq, w_xo, l, x2, norm_ffn_g[l], batch, seq, mem_tokens)
        hid = _matmul(h, w_ff1, l, D_FF, BF16, tm=1024, tn=1024, relu2=True)
        x2 = _matmul(hid, w_ff2_b, l, d, F32, tm=512, tn=512, residual=x2)

    out = _rmsnorm(x2, norm_final_g, x.dtype)
    return out.reshape(batch, seq, d)
```

```python
import functools

import numpy as np
import jax
import jax.numpy as jnp
from jax import lax
from jax.experimental import pallas as pl
from jax.experimental.pallas import tpu as pltpu

F32 = jnp.float32
BF16 = jnp.bfloat16

D_MODEL = 2048
CHUNK = 64
RET_HEADS = 8
RET_DK = D_MODEL // 16
RET_DV = D_MODEL // RET_HEADS
GDN_HEADS = 16
GDN_DK = D_MODEL // GDN_HEADS
GDN_DV = D_MODEL // GDN_HEADS
CONV_K = 4
XATTN_HEADS = 4
XATTN_DH = D_MODEL // XATTN_HEADS
D_FF = 4 * D_MODEL
ROPE_THETA = 10000.0
NORM_EPS = 1e-6
GDN_QKV = 3 * GDN_HEADS * GDN_DK
OFF_RQ = 0
OFF_RK = OFF_RQ + RET_HEADS * RET_DK
OFF_RV = OFF_RK + RET_HEADS * RET_DK
OFF_RG = OFF_RV + RET_HEADS * RET_DV
OFF_GQKV = OFF_RG + RET_HEADS * RET_DV
OFF_GA = OFF_GQKV + GDN_QKV
OFF_GB = OFF_GA + GDN_HEADS
OFF_GZ = OFF_GB + GDN_HEADS
OFF_GATE_A = OFF_GZ + GDN_HEADS * GDN_DV
OFF_GATE_B = OFF_GATE_A + D_MODEL
IN_WIDTH = OFF_GATE_B + D_MODEL

LANE = 128
PROJ_TN = 1024
TAIL_WIDTH = IN_WIDTH - OFF_GZ
RET_BLOCK = 256
GDN_STEP_CHUNKS = 4
VMEM_LIMIT = 56 * 1024 * 1024


def _cparams(sem):
    return pltpu.CompilerParams(dimension_semantics=sem, vmem_limit_bytes=VMEM_LIMIT)


def _dot(a, b):
    return jnp.dot(a, b, preferred_element_type=F32)


def _dot_nt(a, b):
    return lax.dot_general(a, b, (((1,), (1,)), ((), ())), preferred_element_type=F32)


def _dot_tn(a, b):
    return lax.dot_general(a, b, (((0,), (0,)), ((), ())), preferred_element_type=F32)


NEG_LOG2E = -1.4426950408889634


def _sigmoid(x):
    return 1.0 / (1.0 + jnp.exp2(x * NEG_LOG2E))


def _silu(x):
    return x * _sigmoid(x)


def _rmsnorm_kernel(x_ref, g_ref, o_ref):
    x = x_ref[...]
    ms = jnp.mean(x * x, axis=-1, keepdims=True)
    o_ref[...] = (x * lax.rsqrt(ms + NORM_EPS) * g_ref[...]).astype(o_ref.dtype)


def _rmsnorm(x, g, out_dtype, tm=512):
    m, d = x.shape
    tm = min(tm, m)
    return pl.pallas_call(
        _rmsnorm_kernel,
        grid=(m // tm,),
        in_specs=[pl.BlockSpec((tm, d), lambda i: (i, 0)),
                  pl.BlockSpec((1, d), lambda i: (0, 0))],
        out_specs=pl.BlockSpec((tm, d), lambda i: (i, 0)),
        out_shape=jax.ShapeDtypeStruct((m, d), out_dtype),
        compiler_params=_cparams(("parallel",)),
        name="rmsnorm",
    )(x, g.reshape(1, d))


def _mm_kernel(*refs, relu2, has_res, cast_w, w_is_nk):
    it = iter(refs)
    a_ref = next(it)
    w_ref = next(it)
    r_ref = next(it) if has_res else None
    o_ref = next(it)
    wbf_ref = next(it) if cast_w else w_ref

    if cast_w:
        @pl.when(pl.program_id(1) == 0)
        def _():
            if w_is_nk:
                w = w_ref[0].T
            else:
                w = w_ref[...]
            wbf_ref[...] = w.astype(BF16)

    acc = _dot(a_ref[...], wbf_ref[...])
    if relu2:
        acc = jnp.square(jnp.maximum(acc, 0.0))
    if has_res:
        acc = acc + r_ref[...]
    o_ref[...] = acc.astype(o_ref.dtype)


def _matmul(a, w, layer, n_out, out_dtype, tm, tn, relu2=False, residual=None, col0=0,
            w_is_nk=False):
    m, k = a.shape
    assert w.shape[2 if w_is_nk else 1] == k and n_out % tn == 0 and m % tm == 0
    assert col0 + n_out <= w.shape[1 if w_is_nk else 2]
    cast_w = w.dtype != BF16 or w_is_nk
    if w_is_nk:
        assert col0 % 8 == 0
        w_spec = pl.BlockSpec((pl.Element(1), pl.Element(tn), pl.Element(k)),
                              lambda j, i: (layer, pl.multiple_of(col0 + j * tn, 8), 0))
    else:
        assert col0 % tn == 0
        w_spec = pl.BlockSpec((None, k, tn), lambda j, i: (layer, 0, col0 // tn + j))
    in_specs = [pl.BlockSpec((tm, k), lambda j, i: (i, 0)), w_spec]
    args = [a, w]
    if residual is not None:
        in_specs.append(pl.BlockSpec((tm, tn), lambda j, i: (i, j)))
        args.append(residual)
    return pl.pallas_call(
        functools.partial(_mm_kernel, relu2=relu2, has_res=residual is not None, cast_w=cast_w,
                          w_is_nk=w_is_nk),
        grid=(n_out // tn, m // tm),
        in_specs=in_specs,
        out_specs=pl.BlockSpec((tm, tn), lambda j, i: (i, j)),
        out_shape=jax.ShapeDtypeStruct((m, n_out), out_dtype),
        scratch_shapes=[pltpu.VMEM((k, tn), BF16)] if cast_w else [],
        compiler_params=_cparams(("arbitrary", "arbitrary")),
        name="matmul",
    )(*args)


def _rope_kernel(pos_ref, invf_ref, sgn_ref, cos_ref, sin_ref):
    ang = pos_ref[...] * invf_ref[...]
    cos_ref[...] = jnp.cos(ang)
    sin_ref[...] = jnp.sin(ang) * sgn_ref[...]


def _rope_tables(positions):
    t = positions.size
    half = RET_DK // 2
    inv_freq = 1.0 / (ROPE_THETA ** (jnp.arange(0, RET_DK, 2, dtype=F32) / RET_DK))
    invf2 = jnp.concatenate([inv_freq, inv_freq]).reshape(1, RET_DK)
    sgn = jnp.concatenate([-jnp.ones((half,), F32), jnp.ones((half,), F32)]).reshape(1, RET_DK)
    pos = positions.astype(F32).reshape(t, 1)
    tm = 1024
    return pl.pallas_call(
        _rope_kernel,
        grid=(t // tm,),
        in_specs=[pl.BlockSpec((tm, 1), lambda i: (i, 0)),
                  pl.BlockSpec((1, RET_DK), lambda i: (0, 0)),
                  pl.BlockSpec((1, RET_DK), lambda i: (0, 0))],
        out_specs=[pl.BlockSpec((tm, RET_DK), lambda i: (i, 0))] * 2,
        out_shape=[jax.ShapeDtypeStruct((t, RET_DK), F32)] * 2,
        compiler_params=_cparams(("parallel",)),
        name="rope_tables",
    )(pos, invf2, sgn)


def _retention_tables():
    idx = np.arange(RET_BLOCK, dtype=np.float64)
    log_gamma = jnp.log1p(-jnp.exp2(-5.0 - jnp.arange(RET_HEADS, dtype=F32)))
    lg = log_gamma[:, None, None]
    dist = jnp.asarray(np.abs(idx[:, None] - idx[None, :]), F32)
    visible = jnp.asarray((idx[None, :] // CHUNK) <= (idx[:, None] // CHUNK))
    wmat = jnp.where(visible[None], jnp.exp(lg * dist[None]), 0.0)
    qd = jnp.exp(log_gamma[:, None] * jnp.asarray(idx + 1.0, F32))
    kd = jnp.exp(log_gamma[:, None] * jnp.asarray(RET_BLOCK - 1.0 - idx, F32))
    cd = jnp.exp(log_gamma * RET_BLOCK)
    qd = jnp.broadcast_to(qd[:, :, None], (RET_HEADS, RET_BLOCK, RET_DK))
    kd = jnp.broadcast_to(kd[:, :, None], (RET_HEADS, RET_BLOCK, RET_DK))
    cd = jnp.broadcast_to(cd[:, None, None], (RET_HEADS, 8, RET_DV))
    return wmat, qd, kd, cd


def _retention_kernel(q_ref, k_ref, v_ref, rg_ref, cos_ref, sin_ref, w_ref, qd_ref, kd_ref,
                      cd_ref, gn_ref, o_ref, state_ref):
    @pl.when(pl.program_id(1) == 0)
    def _():
        state_ref[...] = jnp.zeros_like(state_ref)

    cos = cos_ref[...]
    sin = sin_ref[...]
    half = RET_DK // 2
    heads = range(RET_HEADS)
    qr, kr, vb = [], [], []
    for h in heads:
        ks = slice(h * RET_DK, (h + 1) * RET_DK)
        q = q_ref[:, ks]
        k = k_ref[:, ks]
        qr.append(q * cos + pltpu.roll(q, half, axis=1) * sin)
        kr.append((k * cos + pltpu.roll(k, half, axis=1) * sin) * (RET_DK ** -0.5))
        vb.append(v_ref[:, h * RET_DV:(h + 1) * RET_DV].astype(BF16))
    scores = [_dot_nt(qr[h].astype(BF16), kr[h].astype(BF16)) * w_ref[h] for h in heads]
    s_prev = [state_ref[h] for h in heads]
    o = [_dot(scores[h].astype(BF16), vb[h])
         + _dot((qr[h] * qd_ref[h]).astype(BF16), s_prev[h].astype(BF16)) for h in heads]
    for h in heads:
        state_ref[h] = (s_prev[h] * cd_ref[h][0:1, :]
                        + _dot_tn((kr[h] * kd_ref[h]).astype(BF16), vb[h]))
    for h in heads:
        vs = slice(h * RET_DV, (h + 1) * RET_DV)
        mu = jnp.mean(o[h], axis=-1, keepdims=True)
        oc = o[h] - mu
        var = jnp.mean(oc * oc, axis=-1, keepdims=True)
        y = oc * lax.rsqrt(var + 1e-5) * gn_ref[:, vs]
        o_ref[:, vs] = (_silu(rg_ref[:, vs]) * y).astype(o_ref.dtype)


def _retention(proj, cos2, sin2, tables, gn_g, batch, seq):
    wmat, qd, kd, cd = tables
    t = batch * seq
    nb = seq // RET_BLOCK
    L = RET_BLOCK
    qk_w = RET_HEADS * RET_DK
    v_w = RET_HEADS * RET_DV
    assert OFF_RQ % qk_w == 0 and OFF_RK % qk_w == 0 and OFF_RV % v_w == 0 and OFF_RG % v_w == 0

    def row(b, s):
        return b * nb + s

    def whole(x):
        return pl.BlockSpec(x.shape, lambda b, s: (0,) * x.ndim)

    gn2 = gn_g.reshape(1, v_w)
    return pl.pallas_call(
        _retention_kernel,
        grid=(batch, nb),
        in_specs=[
            pl.BlockSpec((L, qk_w), lambda b, s: (row(b, s), OFF_RQ // qk_w)),
            pl.BlockSpec((L, qk_w), lambda b, s: (row(b, s), OFF_RK // qk_w)),
            pl.BlockSpec((L, v_w), lambda b, s: (row(b, s), OFF_RV // v_w)),
            pl.BlockSpec((L, v_w), lambda b, s: (row(b, s), OFF_RG // v_w)),
            pl.BlockSpec((L, RET_DK), lambda b, s: (row(b, s), 0)),
            pl.BlockSpec((L, RET_DK), lambda b, s: (row(b, s), 0)),
            whole(wmat), whole(qd), whole(kd), whole(cd), whole(gn2),
        ],
        out_specs=pl.BlockSpec((L, v_w), lambda b, s: (row(b, s), 0)),
        out_shape=jax.ShapeDtypeStruct((t, v_w), BF16),
        scratch_shapes=[pltpu.VMEM((RET_HEADS, RET_DK, RET_DV), F32)],
        compiler_params=_cparams(("arbitrary", "arbitrary")),
        name="retention",
    )(proj, proj, proj, proj, cos2, sin2, wmat, qd, kd, cd, gn2)


def _cumsum_rows(x):
    row = lax.broadcasted_iota(jnp.int32, x.shape, 0)
    s = 1
    while s < x.shape[0]:
        x = x + jnp.where(row >= s, pltpu.roll(x, s, axis=0), 0.0)
        s *= 2
    return x


def _softplus(x):
    return jnp.maximum(x, 0.0) + jnp.log1p(jnp.exp(-jnp.abs(x)))


def _gdn_kernel(qkv_ref, gates_ref, convw_ref, alog_ref, dtb_ref, o_ref, state_ref, xext_ref):
    C = CHUNK
    HALO = 8
    NB = GDN_STEP_CHUNKS
    NP = GDN_HEADS // 2
    DK, DV = GDN_DK, GDN_DV

    @pl.when(pl.program_id(1) == 0)
    def _():
        state_ref[...] = jnp.zeros_like(state_ref)
        xext_ref[0:HALO, :] = jnp.zeros((HALO, GDN_QKV), F32)

    xext_ref[HALO:HALO + NB * C, :] = qkv_ref[...]

    row = lax.broadcasted_iota(jnp.int32, (C, 2 * C), 0)
    lane = lax.broadcasted_iota(jnp.int32, (C, 2 * C), 1)
    col = lane & (C - 1)
    lo = lane < C
    lo_t = lax.broadcasted_iota(jnp.int32, (2 * C, 2 * C), 1) < C
    causal = row >= col
    strict = row > col
    lvl_masks = []
    for b in range(C.bit_length() - 1):
        ib = lax.shift_right_logical(row, b)
        jb = lax.shift_right_logical(col, b)
        lvl_masks.append(((ib & 1) == 1) & (jb == ib - 1))

    def conv_act(ci, col0):
        sl = slice(col0, col0 + LANE)
        r0 = HALO + ci * C
        y = convw_ref[CONV_K - 1:CONV_K, sl] * xext_ref[r0:r0 + C, sl]
        for d in range(1, CONV_K):
            y = y + convw_ref[CONV_K - 1 - d:CONV_K - d, sl] * xext_ref[r0 - d:r0 - d + C, sl]
        return _silu(y)

    def block_diag(x, mask):
        return jnp.concatenate([jnp.where(mask, x, 0.0), jnp.where(mask, 0.0, x)], axis=0).astype(BF16)

    def block_diag_wide(xa, xb):
        za = jnp.zeros_like(xa)
        top = jnp.concatenate([xa, za], axis=1)
        bot = jnp.concatenate([za, xb], axis=1)
        return jnp.concatenate([top, bot], axis=0).astype(BF16)

    def bcast_col(x, j):
        return jnp.broadcast_to(x[:, j:j + 1], (C, LANE))

    chunks = range(NB)
    heads = range(GDN_HEADS)
    items = [(ci, p) for ci in chunks for p in range(NP)]

    cg, cg_t, bt = [], [], []
    for ci in chunks:
        sm = gates_ref[ci * C:(ci + 1) * C, :]
        gl = -jnp.exp(alog_ref[...]) * _softplus(sm + dtb_ref[...])
        cg.append(_cumsum_rows(gl))
        cg_t.append(cg[ci].T)
        bt.append(_sigmoid(sm))

    qn, kn, vv, cgc, btc, ecg, kb = {}, {}, {}, {}, {}, {}, {}
    for ci in chunks:
        for h in heads:
            q = conv_act(ci, h * DK)
            k = conv_act(ci, GDN_HEADS * DK + h * DK)
            vv[ci, h] = conv_act(ci, 2 * GDN_HEADS * DK + h * DV)
            qn[ci, h] = q * lax.rsqrt(jnp.sum(q * q, axis=-1, keepdims=True) + 1e-6) * (DK ** -0.5)
            kn[ci, h] = k * lax.rsqrt(jnp.sum(k * k, axis=-1, keepdims=True) + 1e-6)
    for ci in chunks:
        for h in heads:
            cgc[ci, h] = bcast_col(cg[ci], h)
            btc[ci, h] = bcast_col(bt[ci], GDN_HEADS + h)
            ecg[ci, h] = jnp.exp(cgc[ci, h])
            kb[ci, h] = kn[ci, h] * btc[ci, h]

    decay, knt, m, qk = {}, {}, {}, {}
    for it in items:
        ci, p = it
        a, b = (ci, 2 * p), (ci, 2 * p + 1)
        cgc_p = jnp.where(lo, cgc[a], cgc[b])
        cgr_p = jnp.concatenate([cg_t[ci][2 * p:2 * p + 1, :], cg_t[ci][2 * p + 1:2 * p + 2, :]], axis=1)
        decay[it] = jnp.where(causal, jnp.exp(jnp.where(causal, cgc_p - cgr_p, 0.0)), 0.0)
        knt[it] = jnp.concatenate([kn[a], kn[b]], axis=0).T
    for it in items:
        ci, p = it
        a, b = (ci, 2 * p), (ci, 2 * p + 1)
        lhs = jnp.concatenate([jnp.concatenate([kb[a], kb[b]], axis=1),
                               jnp.concatenate([qn[a], qn[b]], axis=1)], axis=0)
        kq = _dot(lhs.astype(BF16), block_diag(knt[it], lo_t))
        m[it] = jnp.where(strict, kq[:C] * decay[it], 0.0)
        qk[it] = kq[C:] * decay[it]

    r = {it: jnp.where(lvl_masks[0], -m[it], 0.0) for it in items}
    for lm in lvl_masks[1:]:
        cm = {it: jnp.where(lm, m[it], 0.0) for it in items}
        y = {it: cm[it] + _dot(r[it].astype(BF16), block_diag(cm[it], lo)) for it in items}
        r = {it: r[it] - (y[it] + _dot(y[it].astype(BF16), block_diag(r[it], lo))) for it in items}

    u, w = {}, {}
    for it in items:
        ci, p = it
        a, b = (ci, 2 * p), (ci, 2 * p + 1)
        xa = jnp.concatenate([vv[a] * btc[a], kb[a] * ecg[a]], axis=1)
        xb = jnp.concatenate([vv[b] * btc[b], kb[b] * ecg[b]], axis=1)
        rx = _dot(r[it].astype(BF16), block_diag_wide(xa, xb))
        u[it] = (jnp.concatenate([xa[:, :DV], xb[:, :DV]], axis=1) +
                 jnp.concatenate([rx[:, :DV], rx[:, DV + DK:2 * DV + DK]], axis=1))
        w[it] = (jnp.concatenate([xa[:, DV:], xb[:, DV:]], axis=1) +
                 jnp.concatenate([rx[:, DV:DV + DK], rx[:, 2 * DV + DK:]], axis=1))

    state = [state_ref[p] for p in range(NP)]
    for ci in chunks:
        wsqs, v_new = {}, {}
        for p in range(NP):
            it = (ci, p)
            a, b = (ci, 2 * p), (ci, 2 * p + 1)
            qe = jnp.concatenate([qn[a] * ecg[a], qn[b] * ecg[b]], axis=1)
            lhs = jnp.concatenate([w[it], qe], axis=0)
            wsqs[p] = _dot(lhs.astype(BF16), block_diag_wide(state[p][:, :DV], state[p][:, DV:]))
            v_new[p] = u[it] - wsqs[p][:C]
        for p in range(NP):
            it = (ci, p)
            a, b = (ci, 2 * p), (ci, 2 * p + 1)
            g_last_a = cgc[a][C - 1:C, :]
            g_last_b = cgc[b][C - 1:C, :]
            edr = jnp.exp(jnp.concatenate([g_last_a[:, :C] - cg_t[ci][2 * p:2 * p + 1, :],
                                           g_last_b[:, :C] - cg_t[ci][2 * p + 1:2 * p + 2, :]], axis=1))
            lhs = jnp.concatenate([qk[it], knt[it] * edr], axis=0)
            res = _dot(lhs.astype(BF16), block_diag_wide(v_new[p][:, :DV], v_new[p][:, DV:]))
            eg = jnp.exp(jnp.concatenate([g_last_a, g_last_b], axis=1))
            state[p] = state[p] * eg + res[C:]
            o_ref[ci * C:(ci + 1) * C, 2 * p * DV:(2 * p + 2) * DV] = (wsqs[p][C:] + res[:C]).astype(o_ref.dtype)
    for p in range(NP):
        state_ref[p] = state[p]

    xext_ref[0:HALO, :] = xext_ref[NB * C:NB * C + HALO, :]


def _gdn(proj, gates, conv_w, a_log, dt_bias, batch, seq):
    t = batch * seq
    R = GDN_STEP_CHUNKS * CHUNK
    ns = seq // R
    pad = LANE - GDN_HEADS
    alog_p = jnp.pad(a_log.astype(F32), (0, pad)).reshape(1, LANE)
    dtb_p = jnp.pad(dt_bias.astype(F32), (0, pad)).reshape(1, LANE)
    assert OFF_GQKV % GDN_QKV == 0 and seq % R == 0

    def row(b, c):
        return b * ns + c

    return pl.pallas_call(
        _gdn_kernel,
        grid=(batch, ns),
        in_specs=[
            pl.BlockSpec((R, GDN_QKV), lambda b, c: (row(b, c), OFF_GQKV // GDN_QKV)),
            pl.BlockSpec((R, LANE), lambda b, c: (row(b, c), 0)),
            pl.BlockSpec((CONV_K, GDN_QKV), lambda b, c: (0, 0)),
            pl.BlockSpec((1, LANE), lambda b, c: (0, 0)),
            pl.BlockSpec((1, LANE), lambda b, c: (0, 0)),
        ],
        out_specs=pl.BlockSpec((R, D_MODEL), lambda b, c: (row(b, c), 0)),
        out_shape=jax.ShapeDtypeStruct((t, D_MODEL), BF16),
        scratch_shapes=[pltpu.VMEM((GDN_HEADS // 2, GDN_DK, 2 * GDN_DV), F32),
                        pltpu.VMEM((R + 8, GDN_QKV), F32)],
        compiler_params=_cparams(("arbitrary", "arbitrary")),
        name="gdn",
    )(proj, gates, conv_w, alog_p, dtb_p)


def _merge_out_kernel(o_ref, yr_ref, tail_ref, gng_ref, w_ref, x_ref, g_ref, xo_ref, hn_ref):
    DV = GDN_DV
    gng = gng_ref[...]

    def tail_cols(off):
        return tail_ref[:, off - OFF_GZ:off - OFF_GZ + LANE].astype(F32)

    parts = []
    for h in range(GDN_HEADS):
        hs = slice(h * DV, (h + 1) * DV)
        o = o_ref[:, hs].astype(F32)
        on = o * lax.rsqrt(jnp.mean(o * o, axis=-1, keepdims=True) + NORM_EPS) * gng
        yg = _silu(tail_cols(OFF_GZ + h * DV)) * on
        merged = (_sigmoid(tail_cols(OFF_GATE_A + h * DV)) * yr_ref[:, hs].astype(F32)
                  + _sigmoid(tail_cols(OFF_GATE_B + h * DV)) * yg)
        parts.append(merged.astype(BF16))
    x = _dot(jnp.concatenate(parts, axis=1), w_ref[...]) + x_ref[...]
    xo_ref[...] = x
    ms = jnp.mean(x * x, axis=-1, keepdims=True)
    hn_ref[...] = (x * lax.rsqrt(ms + NORM_EPS) * g_ref[...]).astype(hn_ref.dtype)


def _merge_out(o_gdn, yr, tail, gdn_norm_g, w_out_b, layer, x2, next_g, tm=256):
    t, d = x2.shape
    assert t % tm == 0 and w_out_b.dtype == BF16 and tail.shape == (t, TAIL_WIDTH)

    def rows(w):
        return pl.BlockSpec((tm, w), lambda i: (i, 0))

    return pl.pallas_call(
        _merge_out_kernel,
        grid=(t // tm,),
        in_specs=[rows(d), rows(d), rows(TAIL_WIDTH),
                  pl.BlockSpec((1, GDN_DV), lambda i: (0, 0)),
                  pl.BlockSpec((None, d, d), lambda i: (layer, 0, 0)),
                  rows(d),
                  pl.BlockSpec((1, d), lambda i: (0, 0))],
        out_specs=[rows(d), rows(d)],
        out_shape=[jax.ShapeDtypeStruct((t, d), F32), jax.ShapeDtypeStruct((t, d), BF16)],
        compiler_params=_cparams(("parallel",)),
        name="merge_out",
    )(o_gdn, yr, tail, gdn_norm_g.reshape(1, GDN_DV), w_out_b, x2, next_g.reshape(1, d))


def _xa_fold_q_kernel(wq_ref, k_ref, o_ref):
    a = _dot_nt(wq_ref[...].astype(BF16), k_ref[...]) * (XATTN_DH ** -0.5)
    o_ref[...] = a.astype(o_ref.dtype)


def _xa_fold_o_kernel(v_ref, wo_ref, o_ref):
    o_ref[...] = _dot(v_ref[...], wo_ref[...].astype(BF16)).astype(o_ref.dtype)


def _xa_main_kernel(h_ref, a_ref, b_ref, r_ref, g_ref, o_ref, hn_ref, *, mem_tokens):
    s = _dot(h_ref[...], a_ref[...])
    ps = []
    for hh in range(XATTN_HEADS):
        seg = s[:, hh * mem_tokens:(hh + 1) * mem_tokens]
        p = jnp.exp(seg - jnp.max(seg, axis=-1, keepdims=True))
        ps.append((p / jnp.sum(p, axis=-1, keepdims=True)).astype(BF16))
    x = _dot(jnp.concatenate(ps, axis=1), b_ref[...]) + r_ref[...]
    o_ref[...] = x
    ms = jnp.mean(x * x, axis=-1, keepdims=True)
    hn_ref[...] = (x * lax.rsqrt(ms + NORM_EPS) * g_ref[...]).astype(hn_ref.dtype)


def _xattn_residual(h, kv, w_xq, w_xo, layer, x2, next_g, batch, seq, mem_tokens, tq=512):
    t, d = h.shape
    hm = XATTN_HEADS * mem_tokens
    nq = seq // tq
    grid_bh = (batch, XATTN_HEADS)
    fold_q = pl.pallas_call(
        _xa_fold_q_kernel,
        grid=grid_bh,
        in_specs=[pl.BlockSpec((None, d, XATTN_DH), lambda b, hh: (layer, 0, hh)),
                  pl.BlockSpec((mem_tokens, XATTN_DH), lambda b, hh: (b, hh))],
        out_specs=pl.BlockSpec((None, d, mem_tokens), lambda b, hh: (b, 0, hh)),
        out_shape=jax.ShapeDtypeStruct((batch, d, hm), BF16),
        compiler_params=_cparams(("parallel", "parallel")),
        name="xattn_fold_q",
    )(w_xq, kv)
    fold_o = pl.pallas_call(
        _xa_fold_o_kernel,
        grid=grid_bh,
        in_specs=[pl.BlockSpec((mem_tokens, XATTN_DH), lambda b, hh: (b, XATTN_HEADS + hh)),
                  pl.BlockSpec((None, XATTN_DH, d), lambda b, hh: (layer, hh, 0))],
        out_specs=pl.BlockSpec((None, mem_tokens, d), lambda b, hh: (b, hh, 0)),
        out_shape=jax.ShapeDtypeStruct((batch, hm, d), BF16),
        compiler_params=_cparams(("parallel", "parallel")),
        name="xattn_fold_o",
    )(kv, w_xo)
    return pl.pallas_call(
        functools.partial(_xa_main_kernel, mem_tokens=mem_tokens),
        grid=(batch, nq),
        in_specs=[pl.BlockSpec((tq, d), lambda b, i: (b * nq + i, 0)),
                  pl.BlockSpec((None, d, hm), lambda b, i: (b, 0, 0)),
                  pl.BlockSpec((None, hm, d), lambda b, i: (b, 0, 0)),
                  pl.BlockSpec((tq, d), lambda b, i: (b * nq + i, 0)),
                  pl.BlockSpec((1, d), lambda b, i: (0, 0))],
        out_specs=[pl.BlockSpec((tq, d), lambda b, i: (b * nq + i, 0))] * 2,
        out_shape=[jax.ShapeDtypeStruct((t, d), F32), jax.ShapeDtypeStruct((t, d), BF16)],
        compiler_params=_cparams(("parallel", "parallel")),
        name="xattn",
    )(h, fold_q, fold_o, x2, next_g.reshape(1, d))


def kernel(x, mem, positions, norm_mix_g, w_in, conv_w, gdn_a_log, gdn_dt_bias, ret_gn_g, gdn_norm_g, w_out, norm_x_g, norm_mem_g, w_xq, w_xkv, w_xo, norm_ffn_g, w_ff1, w_ff2, norm_final_g):
    batch, seq, d = x.shape
    mem_tokens = mem.shape[1]
    depth = w_in.shape[0]
    t = batch * seq
    x2 = x.reshape(t, d)
    mem2 = mem.reshape(batch * mem_tokens, d)

    cos2, sin2 = _rope_tables(positions)
    ret_tables = _retention_tables()
    w_ff2_b = w_ff2.astype(BF16)
    w_out_b = w_out.astype(BF16)
    w_in_nk = jnp.swapaxes(w_in, 1, 2)

    for l in range(depth):
        h = _rmsnorm(x2, norm_mix_g[l], BF16)
        proj = _matmul(h, w_in_nk, l, OFF_GA, F32, tm=1024, tn=PROJ_TN, w_is_nk=True)
        gates = _matmul(h, w_in_nk, l, LANE, F32, tm=1024, tn=LANE, col0=OFF_GA, w_is_nk=True)
        tail = _matmul(h, w_in_nk, l, TAIL_WIDTH, BF16, tm=1024, tn=PROJ_TN, col0=OFF_GZ, w_is_nk=True)
        yr = _retention(proj, cos2, sin2, ret_tables, ret_gn_g[l], batch, seq)
        o_gdn = _gdn(proj, gates, conv_w[l], gdn_a_log[l], gdn_dt_bias[l], batch, seq)
        x2, h = _merge_out(o_gdn, yr, tail, gdn_norm_g[l], w_out_b, l, x2, norm_x_g[l])
        mem_n = _rmsnorm(mem2, norm_mem_g[l], BF16)
        kv = _matmul(mem_n, w_xkv, l, 2 * d, BF16, tm=batch * mem_tokens, tn=1024)
        x2, h = _xattn_residual(h, kv, w_xq, w_xo, l, x2, norm_ffn_g[l], batch, seq, mem_tokens)
        hid = _matmul(h, w_ff1, l, D_FF, BF16, tm=1024, tn=1024, relu2=True)
        x2 = _matmul(hid, w_ff2_b, l, d, F32, tm=512, tn=512, residual=x2)

    out = _rmsnorm(x2, norm_final_g, x.dtype)
    return out.reshape(batch, seq, d)
```
